```python
import jax, jax.numpy as jnp
from jax import lax
import numpy as np

D_MODEL = 1024
BATCH = 2
SEQ = 8192
DEPTH = 4

N_MIXERS = 4
EPS = 1e-6
D_FF = -(-8 * D_MODEL // (3 * 256)) * 256

GLA_HEADS = 4
GLA_DK = D_MODEL // (2 * GLA_HEADS)
GLA_DV = D_MODEL // GLA_HEADS
GLA_GATE_RANK = 16
GLA_TAU = 16.0
GLA_CHUNK = 64
GLA_IN = 2 * GLA_HEADS * GLA_DK + 2 * GLA_HEADS * GLA_DV + GLA_GATE_RANK

GMLP_D_FFN = 6 * D_MODEL
GMLP_HALF = GMLP_D_FFN // 2
GMLP_GROUPS = 8
GMLP_CHUNK = 128

CONV_WIDTH = 31

NSA_HEADS = 16
NSA_KV_HEADS = 4
NSA_HEAD_DIM = D_MODEL // NSA_HEADS
NSA_CMP_BLOCK = 32
NSA_CMP_STRIDE = 16
NSA_CMP_HIDDEN = 256
NSA_SEL_BLOCK = 64
NSA_TOP_N = 16
NSA_WINDOW = 512
NSA_Q_BLOCK = 128
NSA_IN = NSA_HEADS * NSA_HEAD_DIM + 6 * NSA_KV_HEADS * NSA_HEAD_DIM + 3 * NSA_HEADS

kernel_name = 'hybrid_gla_gmlp_conformer_nsa_trunk'


def n_layers_of(m):
    return len(range(m, DEPTH, N_MIXERS))


def rms_norm(x, g):
    xf = x.astype(jnp.float32)
    y = xf * lax.rsqrt(jnp.mean(xf * xf, axis=-1, keepdims=True) + EPS)
    return y.astype(x.dtype) * g


def layer_norm(x, g, b):
    xf = x.astype(jnp.float32)
    mu = jnp.mean(xf, axis=-1, keepdims=True)
    var = jnp.mean(jnp.square(xf - mu), axis=-1, keepdims=True)
    return ((xf - mu) * lax.rsqrt(var + EPS)).astype(x.dtype) * g + b


def masked_softmax(s, mask):
    s = jnp.where(mask, s.astype(jnp.float32), -1e30)
    return jnp.where(mask, jax.nn.softmax(s, axis=-1), 0.0)


def alibi_slopes(n):
    return 2.0 ** (-8.0 * jnp.arange(1, n + 1, dtype=jnp.float32) / n)


def swiglu(h, w_gate, w_up, w_down):
    return (jax.nn.silu(h @ w_gate) * (h @ w_up)) @ w_down


def gla_mixer(h, w_in, w_gate2, b_gate, g_norm, w_out):
    B, S, _ = h.shape
    H, dk, dv, C = GLA_HEADS, GLA_DK, GLA_DV, GLA_CHUNK
    proj = h @ w_in
    o1, o2, o3, o4 = H * dk, 2 * H * dk, 2 * H * dk + H * dv, 2 * H * dk + 2 * H * dv
    q = proj[..., :o1] * dk ** -0.5
    k = proj[..., o1:o2]
    v = proj[..., o2:o3]
    g_out = proj[..., o3:o4]
    a_lr = proj[..., o4:]
    log_a = jax.nn.log_sigmoid((a_lr @ w_gate2 + b_gate).astype(jnp.float32)) / GLA_TAU

    def to_chunks(t, d):
        return t.reshape(B, S // C, C, H, d).transpose(1, 0, 3, 2, 4)

    qs, ks, vs, ls = to_chunks(q, dk), to_chunks(k, dk), to_chunks(v, dv), to_chunks(log_a, dk)
    causal = jnp.tril(jnp.ones((C, C), dtype=bool))

    def step(state, inp):
        qc, kc, vc, lc = inp
        bcum = jnp.cumsum(lc, axis=2)
        o_inter = jnp.einsum('bhcd,bhde->bhce', qc * jnp.exp(bcum), state)
        diff = bcum[:, :, :, None, :] - bcum[:, :, None, :, :]
        decay = jnp.exp(jnp.where(causal[:, :, None], diff, -jnp.inf))
        attn = jnp.einsum('bhtd,bhsd,bhtsd->bhts', qc, kc, decay)
        o = o_inter + jnp.einsum('bhts,bhse->bhte', attn, vc)
        blast = bcum[:, :, -1:, :]
        k_dec = kc * jnp.exp(blast - bcum)
        state = jnp.exp(blast[:, :, 0, :])[..., None] * state + jnp.einsum('bhcd,bhce->bhde', k_dec, vc)
        return state, o

    state0 = jnp.zeros((B, H, dk, dv), jnp.float32)
    _, o = lax.scan(step, state0, (qs, ks, vs, ls))
    o = o.transpose(1, 0, 3, 2, 4).reshape(B, S, H, dv).astype(h.dtype)
    o = rms_norm(o, g_norm) * jax.nn.silu(g_out.reshape(B, S, H, dv))
    return o.reshape(B, S, H * dv) @ w_out


def gmlp_mixer(h, w_in, b_in, ln_g, ln_b, w_s, b_s, w_out):
    B, S, _ = h.shape
    G, C, E = GMLP_GROUPS, GMLP_CHUNK, GMLP_HALF
    z = jax.nn.gelu(h @ w_in + b_in)
    u, v = z[..., :E], z[..., E:]
    v = layer_norm(v, ln_g, ln_b)
    vc = v.reshape(B, S // C, C, G, E // G)
    ws = w_s * jnp.tril(jnp.ones((C, C), dtype=w_s.dtype))
    sv = jnp.einsum('gts,bnsgc->bntgc', ws, vc) + b_s.T[None, None, :, :, None]
    return (u * sv.reshape(B, S, E)) @ w_out


def conv_mixer(h, w_pw1, b_pw1, w_dw, b_dw, ln_g, ln_b, w_pw2, b_pw2):
    D = h.shape[-1]
    ag = h @ w_pw1 + b_pw1
    y = ag[..., :D] * jax.nn.sigmoid(ag[..., D:])
    y = lax.conv_general_dilated(
        y, w_dw[:, None, :], window_strides=(1,), padding=[(CONV_WIDTH - 1, 0)],
        dimension_numbers=('NWC', 'WIO', 'NWC'), feature_group_count=D) + b_dw
    y = jax.nn.silu(layer_norm(y, ln_g, ln_b))
    return y @ w_pw2 + b_pw2


def nsa_mixer(h, w_in, pe_k, w_ck1, w_ck2, pe_v, w_cv1, w_cv2, w_out):
    B, S, _ = h.shape
    H, G, dh = NSA_HEADS, NSA_KV_HEADS, NSA_HEAD_DIM
    R = H // G
    QB, W, SEL, STR = NSA_Q_BLOCK, NSA_WINDOW, NSA_SEL_BLOCK, NSA_CMP_STRIDE
    Nc = S // STR - 1
    Ns = S // SEL
    top_n = min(NSA_TOP_N, Ns)

    proj = h @ w_in
    q = (proj[..., :H * dh] * dh ** -0.5).reshape(B, S, G, R, dh).transpose(0, 2, 3, 1, 4)
    kv = proj[..., H * dh:H * dh + 6 * G * dh].reshape(B, S, 6, G, dh).transpose(2, 0, 3, 1, 4)
    gates = jax.nn.sigmoid(proj[..., H * dh + 6 * G * dh:].astype(jnp.float32))
    gates = gates.reshape(B, S, G, R, 3).transpose(0, 2, 3, 1, 4)

    def compress(t, pe, w1, w2):
        tc = t.reshape(B, G, S // STR, STR, dh)
        blocks = jnp.concatenate([tc[:, :, :-1], tc[:, :, 1:]], axis=3) + pe
        return jax.nn.gelu(blocks.reshape(B, G, Nc, NSA_CMP_BLOCK * dh) @ w1) @ w2

    k_cmp = compress(kv[0], pe_k, w_ck1, w_ck2)
    v_cmp = compress(kv[1], pe_v, w_cv1, w_cv2)
    k_sel_blocks = kv[2].reshape(B, G, Ns, SEL, dh)
    v_sel_blocks = kv[3].reshape(B, G, Ns, SEL, dh)
    k_win = jnp.pad(kv[4], ((0, 0), (0, 0), (W, 0), (0, 0)))
    v_win = jnp.pad(kv[5], ((0, 0), (0, 0), (W, 0), (0, 0)))

    sl = alibi_slopes(H).reshape(G, R)[None, :, :, None, None]
    c_start = STR * jnp.arange(Nc, dtype=jnp.int32)
    c_last = c_start + NSA_CMP_BLOCK - 1
    c_center = c_start.astype(jnp.float32) + (NSA_CMP_BLOCK - 1) / 2.0
    s_start = SEL * jnp.arange(Ns, dtype=jnp.int32)
    overlap = jnp.clip(jnp.minimum(c_start[:, None] + NSA_CMP_BLOCK, s_start[None] + SEL)
                       - jnp.maximum(c_start[:, None], s_start[None]), 0, None).astype(jnp.float32) / NSA_CMP_BLOCK
    blk = jnp.arange(Ns, dtype=jnp.int32)
    gather = jax.vmap(jax.vmap(lambda kb, ix: kb[ix]))

    def q_block(n):
        q0 = n * QB
        qb = lax.dynamic_slice_in_dim(q, q0, QB, axis=3)
        gb = lax.dynamic_slice_in_dim(gates, q0, QB, axis=3)
        t = q0 + jnp.arange(QB, dtype=jnp.int32)
        tf = t.astype(jnp.float32)
        s_c = jnp.einsum('bgrtd,bgnd->bgrtn', qb, k_cmp).astype(jnp.float32)
        s_c = s_c - sl * (tf[:, None] - c_center[None, :])
        p_c = masked_softmax(s_c, c_last[None, :] <= t[:, None])
        o_c = jnp.einsum('bgrtn,bgnd->bgrtd', p_c, v_cmp)
        imp = jnp.einsum('bgrtn,nj->bgtj', p_c, overlap)
        cur = t // SEL
        forced = (blk[None] == 0) | (blk[None] == cur[:, None]) | (blk[None] == cur[:, None] - 1)
        imp = jnp.where(forced, jnp.inf, jnp.where(blk[None] * SEL <= t[:, None], imp, -jnp.inf))
        _, idx = lax.top_k(imp, top_n)
        k_sel = gather(k_sel_blocks, idx).reshape(B, G, QB, top_n * SEL, dh)
        v_sel = gather(v_sel_blocks, idx).reshape(B, G, QB, top_n * SEL, dh)
        pos = (idx[..., None] * SEL + jnp.arange(SEL, dtype=jnp.int32)).reshape(B, G, QB, top_n * SEL)
        dist_s = t[None, None, :, None] - pos
        s_s = jnp.einsum('bgrtd,bgtkd->bgrtk', qb, k_sel).astype(jnp.float32)
        s_s = s_s - sl * dist_s[:, :, None].astype(jnp.float32)
        p_s = masked_softmax(s_s, (dist_s >= 0)[:, :, None])
        o_s = jnp.einsum('bgrtk,bgtkd->bgrtd', p_s, v_sel)
        kw = lax.dynamic_slice_in_dim(k_win, q0, W + QB, axis=2)
        vw = lax.dynamic_slice_in_dim(v_win, q0, W + QB, axis=2)
        s_pos = q0 - W + jnp.arange(W + QB, dtype=jnp.int32)
        dist_w = t[:, None] - s_pos[None]
        valid_w = (dist_w >= 0) & (dist_w < W) & (s_pos[None] >= 0)
        s_w = jnp.einsum('bgrtd,bgsd->bgrts', qb, kw).astype(jnp.float32)
        s_w = s_w - sl * dist_w.astype(jnp.float32)
        p_w = masked_softmax(s_w, valid_w)
        o_w = jnp.einsum('bgrts,bgsd->bgrtd', p_w, vw)
        return gb[..., 0:1] * o_c + gb[..., 1:2] * o_s + gb[..., 2:3] * o_w

    outs = lax.map(q_block, jnp.arange(S // QB, dtype=jnp.int32))
    o = outs.transpose(1, 0, 4, 2, 3, 5).reshape(B, S, H * dh).astype(h.dtype)
    return o @ w_out


def setup_inputs(seed: int = 0) -> dict:
    key = jax.random.key(seed)
    keys = iter(jax.random.split(key, 64))
    D = D_MODEL
    out_scale = (2.0 * DEPTH) ** -0.5

    def w(shape, fan_in, scale=1.0):
        return jax.random.normal(next(keys), shape, jnp.float32) * (scale * fan_in ** -0.5)

    def gain(shape):
        return 1.0 + 0.05 * jax.random.normal(next(keys), shape, jnp.float32)

    def bias(shape, s=0.02):
        return s * jax.random.normal(next(keys), shape, jnp.float32)

    nA, nB, nC, nD = n_layers_of(0), n_layers_of(1), n_layers_of(2), n_layers_of(3)
    dh = NSA_HEAD_DIM
    return {
        'x': jax.random.normal(next(keys), (BATCH, SEQ, D), jnp.float32),
        'norm_mix': gain((DEPTH, D)),
        'norm_ffn': gain((DEPTH, D)),
        'norm_final': gain((D,)),
        'w_ffn_gate': w((DEPTH, D, D_FF), D),
        'w_ffn_up': w((DEPTH, D, D_FF), D),
        'w_ffn_down': w((DEPTH, D_FF, D), D_FF, out_scale),
        'a_w_in': w((nA, D, GLA_IN), D),
        'a_w_gate2': w((nA, GLA_GATE_RANK, GLA_HEADS * GLA_DK), GLA_GATE_RANK),
        'a_b_gate': bias((nA, GLA_HEADS * GLA_DK), 0.5),
        'a_g_norm': gain((nA, GLA_DV)),
        'a_w_out': w((nA, GLA_HEADS * GLA_DV, D), GLA_HEADS * GLA_DV, out_scale),
        'b_w_in': w((nB, D, GMLP_D_FFN), D),
        'b_b_in': bias((nB, GMLP_D_FFN)),
        'b_ln_g': gain((nB, GMLP_HALF)),
        'b_ln_b': bias((nB, GMLP_HALF)),
        'b_w_s': w((nB, GMLP_GROUPS, GMLP_CHUNK, GMLP_CHUNK), GMLP_CHUNK, 0.5),
        'b_b_s': gain((nB, GMLP_GROUPS, GMLP_CHUNK)),
        'b_w_out': w((nB, GMLP_HALF, D), GMLP_HALF, out_scale),
        'c_w_pw1': w((nC, D, 2 * D), D),
        'c_b_pw1': bias((nC, 2 * D)),
        'c_w_dw': w((nC, CONV_WIDTH, D), CONV_WIDTH),
        'c_b_dw': bias((nC, D)),
        'c_ln_g': gain((nC, D)),
        'c_ln_b': bias((nC, D)),
        'c_w_pw2': w((nC, D, D), D, out_scale),
        'c_b_pw2': bias((nC, D)),
        'd_w_in': w((nD, D, NSA_IN), D),
        'd_pe_k': bias((nD, NSA_CMP_BLOCK, dh)),
        'd_w_ck1': w((nD, NSA_CMP_BLOCK * dh, NSA_CMP_HIDDEN), NSA_CMP_BLOCK * dh),
        'd_w_ck2': w((nD, NSA_CMP_HIDDEN, dh), NSA_CMP_HIDDEN),
        'd_pe_v': bias((nD, NSA_CMP_BLOCK, dh)),
        'd_w_cv1': w((nD, NSA_CMP_BLOCK * dh, NSA_CMP_HIDDEN), NSA_CMP_BLOCK * dh),
        'd_w_cv2': w((nD, NSA_CMP_HIDDEN, dh), NSA_CMP_HIDDEN),
        'd_w_out': w((nD, NSA_HEADS * dh, D), NSA_HEADS * dh, out_scale),
    }


def reference(x, norm_mix, norm_ffn, norm_final, w_ffn_gate, w_ffn_up, w_ffn_down,
              a_w_in, a_w_gate2, a_b_gate, a_g_norm, a_w_out,
              b_w_in, b_b_in, b_ln_g, b_ln_b, b_w_s, b_b_s, b_w_out,
              c_w_pw1, c_b_pw1, c_w_dw, c_b_dw, c_ln_g, c_ln_b, c_w_pw2, c_b_pw2,
              d_w_in, d_pe_k, d_w_ck1, d_w_ck2, d_pe_v, d_w_cv1, d_w_cv2, d_w_out):
    h = x
    for i in range(DEPTH):
        m, j = i % N_MIXERS, i // N_MIXERS
        y = rms_norm(h, norm_mix[i])
        if m == 0:
            y = gla_mixer(y, a_w_in[j], a_w_gate2[j], a_b_gate[j], a_g_norm[j], a_w_out[j])
        elif m == 1:
            y = gmlp_mixer(y, b_w_in[j], b_b_in[j], b_ln_g[j], b_ln_b[j], b_w_s[j], b_b_s[j], b_w_out[j])
        elif m == 2:
            y = conv_mixer(y, c_w_pw1[j], c_b_pw1[j], c_w_dw[j], c_b_dw[j], c_ln_g[j], c_ln_b[j],
                           c_w_pw2[j], c_b_pw2[j])
        else:
            y = nsa_mixer(y, d_w_in[j], d_pe_k[j], d_w_ck1[j], d_w_ck2[j], d_pe_v[j], d_w_cv1[j],
                          d_w_cv2[j], d_w_out[j])
        h = h + y
        h = h + swiglu(rms_norm(h, norm_ffn[i]), w_ffn_gate[i], w_ffn_up[i], w_ffn_down[i])
    return rms_norm(h, norm_final)
```

```python
import functools

import jax
import jax.numpy as jnp
from jax import lax
from jax.experimental import pallas as pl
from jax.experimental.pallas import tpu as pltpu

F32 = jnp.float32
BF16 = jnp.bfloat16

EPS = 1e-6
V7X_VMEM_BYTES = 64 * 1024 * 1024
VMEM_LIMIT_BYTES = V7X_VMEM_BYTES - 8 * 1024 * 1024
LANES = 128

GLA_HEADS = 4
GLA_TAU = 16.0
GLA_CHUNK = 64
GLA_SUB = 16
GMLP_GROUPS = 8
GMLP_CHUNK = 128
CONV_WIDTH = 31
CONV_HALO = 32
NSA_HEADS = 16
NSA_KV_HEADS = 4
NSA_HEAD_DIM = 64
NSA_CMP_BLOCK = 32
NSA_CMP_STRIDE = 16
NSA_SEL_BLOCK = 64
NSA_TOP_N = 16
NSA_WINDOW = 512
NSA_Q_BLOCK = 128
NEG_BIG = -1e30

NT_DIMS = (((1,), (1,)), ((), ()))
TN_DIMS = (((0,), (0,)), ((), ()))


def _params(*sem):
    return pltpu.CompilerParams(dimension_semantics=sem, vmem_limit_bytes=VMEM_LIMIT_BYTES)


def _dot(a, b):
    return jnp.dot(a, b, preferred_element_type=F32)


def _dot_nt(a, b):
    return lax.dot_general(a, b, NT_DIMS, preferred_element_type=F32)


def _dot_tn(a, b):
    return lax.dot_general(a, b, TN_DIMS, preferred_element_type=F32)


def _sigmoid(x):
    return 1.0 / (1.0 + jnp.exp(-x))


def _silu(x):
    return x * _sigmoid(x)


def _gelu_tanh(x):
    return 0.5 * x * (1.0 + jnp.tanh(0.7978845608028654 * (x + 0.044715 * (x * x * x))))


def _log_sigmoid(x):
    return jnp.minimum(x, 0.0) - jnp.log(1.0 + jnp.exp(-jnp.abs(x)))


def _rms(x, gain):
    ms = jnp.mean(x * x, axis=-1, keepdims=True)
    return x * lax.rsqrt(ms + EPS) * gain


def _layer_norm(x, gain, bias):
    mu = jnp.mean(x, axis=-1, keepdims=True)
    xc = x - mu
    var = jnp.mean(xc * xc, axis=-1, keepdims=True)
    return xc * lax.rsqrt(var + EPS) * gain + bias


def _split_bf16(x):
    hi = x.astype(BF16)
    lo = (x - hi.astype(F32)).astype(BF16)
    return hi, lo


def _row(v):
    return v.reshape(1, -1)


def _linear_kernel(*refs, has_gain, has_bias, has_res, act):
    refs = list(refs)
    x_ref = refs.pop(0)
    g_ref = refs.pop(0) if has_gain else None
    w_ref = refs.pop(0)
    b_ref = refs.pop(0) if has_bias else None
    r_ref = refs.pop(0) if has_res else None
    o_ref = refs.pop(0)
    xn_ref = refs.pop(0) if has_gain else None

    if has_gain:
        @pl.when(pl.program_id(1) == 0)
        def _():
            xn_ref[...] = _rms(x_ref[...].astype(F32), g_ref[...]).astype(BF16)
        xn = xn_ref[...]
    else:
        xn = x_ref[...].astype(BF16)
    y = _dot(xn, w_ref[...])
    if has_bias:
        y = y + b_ref[...]
    if act == "gelu":
        y = _gelu_tanh(y)
    elif act == "sigmoid":
        y = _sigmoid(y)
    if has_res:
        y = y + r_ref[...]
    o_ref[...] = y.astype(o_ref.dtype)


def _pick(n, pref):
    for c in pref:
        if n % c == 0:
            return c
    return n


def _linear(x, w, *, gain=None, bias=None, residual=None, act=None, out_dtype=F32, tm=512, tn=None):
    m, k = x.shape
    n = w.shape[1]
    tm = min(tm, m)
    if tn is None:
        tn = _pick(n, (512, 256, 128))
    grid = (m // tm, n // tn)
    in_specs = [pl.BlockSpec((tm, k), lambda i, j: (i, 0))]
    args = [x]
    if gain is not None:
        in_specs.append(pl.BlockSpec((1, k), lambda i, j: (0, 0)))
        args.append(_row(gain))
    in_specs.append(pl.BlockSpec((k, tn), lambda i, j: (0, j)))
    args.append(w.astype(BF16))
    if bias is not None:
        in_specs.append(pl.BlockSpec((1, tn), lambda i, j: (0, j)))
        args.append(_row(bias))
    if residual is not None:
        in_specs.append(pl.BlockSpec((tm, tn), lambda i, j: (i, j)))
        args.append(residual)
    scratch = [pltpu.VMEM((tm, k), BF16)] if gain is not None else []
    kern = functools.partial(_linear_kernel, has_gain=gain is not None, has_bias=bias is not None,
                             has_res=residual is not None, act=act)
    return pl.pallas_call(
        kern, grid=grid, in_specs=in_specs,
        out_specs=pl.BlockSpec((tm, tn), lambda i, j: (i, j)),
        out_shape=jax.ShapeDtypeStruct((m, n), out_dtype),
        scratch_shapes=scratch, compiler_params=_params("parallel", "arbitrary"),
        name="linear",
    )(*args)


def _ffn_kernel(*refs, has_final):
    refs = list(refs)
    h_ref, g_ref, wg_ref, wu_ref, wd_ref = refs[:5]
    fg_ref = refs[5] if has_final else None
    o_ref, xn_ref, acc_ref = refs[-3:]
    k = pl.program_id(1)

    @pl.when(k == 0)
    def _():
        xn_ref[...] = _rms(h_ref[...], g_ref[...]).astype(BF16)
        acc_ref[...] = jnp.zeros_like(acc_ref)

    xn = xn_ref[...]
    gate = _dot(xn, wg_ref[...])
    up = _dot(xn, wu_ref[...])
    a = (_silu(gate) * up).astype(BF16)
    acc_ref[...] += _dot(a, wd_ref[...])

    @pl.when(k == pl.num_programs(1) - 1)
    def _():
        y = h_ref[...] + acc_ref[...]
        if has_final:
            y = _rms(y, fg_ref[...])
        o_ref[...] = y


def _ffn(h, gain, w_gate, w_up, w_down, final_gain=None, tm=512, tf=256):
    m, d = h.shape
    ff = w_gate.shape[1]
    tm = min(tm, m)
    grid = (m // tm, ff // tf)
    in_specs = [
        pl.BlockSpec((tm, d), lambda i, k: (i, 0)),
        pl.BlockSpec((1, d), lambda i, k: (0, 0)),
        pl.BlockSpec((d, tf), lambda i, k: (0, k)),
        pl.BlockSpec((d, tf), lambda i, k: (0, k)),
        pl.BlockSpec((tf, d), lambda i, k: (k, 0)),
    ]
    args = [h, _row(gain), w_gate.astype(BF16), w_up.astype(BF16), w_down.astype(BF16)]
    if final_gain is not None:
        in_specs.append(pl.BlockSpec((1, d), lambda i, k: (0, 0)))
        args.append(_row(final_gain))
    return pl.pallas_call(
        functools.partial(_ffn_kernel, has_final=final_gain is not None),
        grid=grid, in_specs=in_specs,
        out_specs=pl.BlockSpec((tm, d), lambda i, k: (i, 0)),
        out_shape=jax.ShapeDtypeStruct((m, d), F32),
        scratch_shapes=[pltpu.VMEM((tm, d), BF16), pltpu.VMEM((tm, d), F32)],
        compiler_params=_params("parallel", "arbitrary"),
        name="ffn",
    )(*args)


def _gla_decay_kernel(h_ref, g_ref, w1_ref, w2_ref, b_ref, o_ref):
    xn = _rms(h_ref[...], g_ref[...]).astype(BF16)
    lr = _dot(xn, w1_ref[...]).astype(BF16)
    x = _dot(lr, w2_ref[...]) + b_ref[...]
    o_ref[...] = _log_sigmoid(x) * (1.0 / GLA_TAU)


def _gla_decay(h, gain, w_lr, w_gate2, b_gate, tm=512):
    m, d = h.shape
    rank = w_lr.shape[1]
    n = w_gate2.shape[1]
    tm = min(tm, m)
    w1 = jnp.zeros((d, LANES), BF16).at[:, :rank].set(w_lr.astype(BF16))
    w2 = jnp.zeros((LANES, n), BF16).at[:rank, :].set(w_gate2.astype(BF16))
    return pl.pallas_call(
        _gla_decay_kernel, grid=(m // tm,),
        in_specs=[
            pl.BlockSpec((tm, d), lambda i: (i, 0)),
            pl.BlockSpec((1, d), lambda i: (0, 0)),
            pl.BlockSpec((d, LANES), lambda i: (0, 0)),
            pl.BlockSpec((LANES, n), lambda i: (0, 0)),
            pl.BlockSpec((1, n), lambda i: (0, 0)),
        ],
        out_specs=pl.BlockSpec((tm, n), lambda i: (i, 0)),
        out_shape=jax.ShapeDtypeStruct((m, n), F32),
        compiler_params=_params("parallel"),
        name="gla_decay",
    )(h, _row(gain), w1, w2, _row(b_gate))


def _gla_core_kernel(q_ref, k_ref, v_ref, go_ref, la_ref, gn_ref, o_ref, st_ref, *, n_chunks, scale):
    c_len, sub = GLA_CHUNK, GLA_SUB

    @pl.when(pl.program_id(2) == 0)
    def _():
        st_ref[...] = jnp.zeros_like(st_ref)

    r_io = lax.broadcasted_iota(jnp.int32, (c_len, c_len), 0)
    c_io = lax.broadcasted_iota(jnp.int32, (c_len, c_len), 1)
    tril = jnp.where(r_io >= c_io, 1.0, 0.0).astype(BF16)
    gn = gn_ref[...]

    def chunk(c, carry):
        rows = pl.ds(pl.multiple_of(c * c_len, c_len), c_len)
        hi, lo = _split_bf16(la_ref[rows, :])
        bcum = _dot(tril, hi) + _dot(tril, lo)
        q = q_ref[rows, :] * scale
        k = k_ref[rows, :]
        v = v_ref[rows, :].astype(BF16)
        st = st_ref[...]
        o = _dot_nt((q * jnp.exp(bcum)).astype(BF16), st.astype(BF16))
        parts = []
        for i in range(c_len // sub):
            r0, r1 = i * sub, (i + 1) * sub
            base = bcum[r0 - 1:r0, :] if i else jnp.zeros((1, bcum.shape[1]), F32)
            qi = (q[r0:r1] * jnp.exp(bcum[r0:r1] - base)).astype(BF16)
            ki = (k[:r1] * jnp.exp(base - bcum[:r1])).astype(BF16)
            a = _dot_nt(qi, ki)
            causal = (lax.broadcasted_iota(jnp.int32, (sub, r1), 1)
                      <= lax.broadcasted_iota(jnp.int32, (sub, r1), 0) + r0)
            a = jnp.where(causal, a, 0.0).astype(BF16)
            parts.append(_dot(a, v[:r1]))
        o = o + jnp.concatenate(parts, axis=0)
        g = go_ref[rows, :]
        o_ref[rows, :] = (_rms(o, gn) * _silu(g)).astype(o_ref.dtype)
        blast = bcum[c_len - 1:c_len, :]
        kdec = (k * jnp.exp(blast - bcum)).astype(BF16)
        st_ref[...] = st * jnp.exp(blast) + _dot_tn(v, kdec)
        return carry

    lax.fori_loop(0, n_chunks, chunk, 0)


def _gla_core(proj, log_a, g_norm, batch, seq, tr=512):
    t = proj.shape[0]
    h_n = GLA_HEADS
    dk = log_a.shape[1] // h_n
    dv = (proj.shape[1] - 2 * h_n * dk) // (2 * h_n)
    tr = min(tr, seq)
    nt = seq // tr
    kb, vb, gb = h_n, (2 * h_n * dk) // dv, (2 * h_n * dk) // dv + h_n
    kern = functools.partial(_gla_core_kernel, n_chunks=tr // GLA_CHUNK, scale=dk ** -0.5)
    return pl.pallas_call(
        kern, grid=(batch, h_n, nt),
        in_specs=[
            pl.BlockSpec((tr, dk), lambda b, h, i: (b * nt + i, h)),
            pl.BlockSpec((tr, dk), lambda b, h, i: (b * nt + i, kb + h)),
            pl.BlockSpec((tr, dv), lambda b, h, i: (b * nt + i, vb + h)),
            pl.BlockSpec((tr, dv), lambda b, h, i: (b * nt + i, gb + h)),
            pl.BlockSpec((tr, dk), lambda b, h, i: (b * nt + i, h)),
            pl.BlockSpec((1, dv), lambda b, h, i: (0, 0)),
        ],
        out_specs=pl.BlockSpec((tr, dv), lambda b, h, i: (b * nt + i, h)),
        out_shape=jax.ShapeDtypeStruct((t, h_n * dv), BF16),
        scratch_shapes=[pltpu.VMEM((dv, dk), F32)],
        compiler_params=_params("parallel", "parallel", "arbitrary"),
        name="gla_core",
    )(proj, proj, proj, proj, log_a, _row(g_norm))


def _gla_mixer(h, gain, w_in, w_gate2, b_gate, g_norm, w_out, batch, seq):
    n_main = w_in.shape[1] - w_gate2.shape[0]
    proj = _linear(h, w_in[:, :n_main], gain=gain)
    log_a = _gla_decay(h, gain, w_in[:, n_main:], w_gate2, b_gate)
    o = _gla_core(proj, log_a, g_norm, batch, seq)
    return _linear(o, w_out, residual=h)


def _gmlp_sgu_kernel(u_ref, v_ref, lg_ref, lb_ref, ws_ref, bs_ref, wo_ref, h_ref, o_ref, p_ref, *, n_chunks):
    c_len = GMLP_CHUNK
    groups = ws_ref.shape[0]
    e = v_ref.shape[1]
    ge = e // groups
    causal = (lax.broadcasted_iota(jnp.int32, (c_len, c_len), 0)
              >= lax.broadcasted_iota(jnp.int32, (c_len, c_len), 1))
    for c in range(n_chunks):
        rows = slice(c * c_len, (c + 1) * c_len)
        vn = _layer_norm(v_ref[rows, :].astype(F32), lg_ref[...], lb_ref[...]).astype(BF16)
        for g in range(groups):
            cols = slice(g * ge, (g + 1) * ge)
            ws = jnp.where(causal, ws_ref[g], 0.0).astype(BF16)
            sv = _dot(ws, vn[:, cols]) + bs_ref[:, cols]
            p_ref[rows, cols] = (u_ref[rows, cols].astype(F32) * sv).astype(BF16)
    o_ref[...] = h_ref[...] + _dot(p_ref[...], wo_ref[...])


def _gmlp_mixer(h, gain, w_in, b_in, ln_g, ln_b, w_s, b_s, w_out, tm=256):
    m, d = h.shape
    e = w_out.shape[0]
    groups, c_len = w_s.shape[0], w_s.shape[1]
    z = _linear(h, w_in, gain=gain, bias=b_in, act="gelu")
    tm = min(tm, m)
    bs_full = jnp.repeat(b_s.T, e // groups, axis=1)
    return pl.pallas_call(
        functools.partial(_gmlp_sgu_kernel, n_chunks=tm // c_len),
        grid=(m // tm,),
        in_specs=[
            pl.BlockSpec((tm, e), lambda i: (i, 0)),
            pl.BlockSpec((tm, e), lambda i: (i, 1)),
            pl.BlockSpec((1, e), lambda i: (0, 0)),
            pl.BlockSpec((1, e), lambda i: (0, 0)),
            pl.BlockSpec((groups, c_len, c_len), lambda i: (0, 0, 0)),
            pl.BlockSpec((c_len, e), lambda i: (0, 0)),
            pl.BlockSpec((e, d), lambda i: (0, 0)),
            pl.BlockSpec((tm, d), lambda i: (i, 0)),
        ],
        out_specs=pl.BlockSpec((tm, d), lambda i: (i, 0)),
        out_shape=jax.ShapeDtypeStruct((m, d), F32),
        scratch_shapes=[pltpu.VMEM((tm, e), BF16)],
        compiler_params=_params("parallel"),
        name="gmlp_sgu",
    )(z, z, _row(ln_g), _row(ln_b), w_s, bs_full, w_out.astype(BF16), h)


def _conv_kernel(h_ref, halo_ref, g_ref, w1_ref, b1_ref, wdw_ref, bdw_ref, lg_ref, lb_ref, w2_ref, b2_ref,
                 o_ref, y_ref, z_ref, *, tm, rc):
    d = h_ref.shape[1]
    halo = CONV_HALO

    def glu(x):
        ag = _dot(_rms(x, g_ref[...]).astype(BF16), w1_ref[...]) + b1_ref[...]
        return ag[:, :d] * _sigmoid(ag[:, d:])

    first = pl.program_id(1) == 0
    y_ref[0:halo, :] = jnp.where(first, 0.0, glu(halo_ref[...]))
    y_ref[halo:, :] = glu(h_ref[...])
    off = halo - (CONV_WIDTH - 1)
    for r in range(tm // rc):
        acc = jnp.zeros((rc, d), F32)
        for j in range(CONV_WIDTH):
            s = r * rc + off + j
            acc = acc + y_ref[s:s + rc, :] * wdw_ref[j:j + 1, :]
        yc = _layer_norm(acc + bdw_ref[...], lg_ref[...], lb_ref[...])
        z_ref[r * rc:(r + 1) * rc, :] = _silu(yc).astype(BF16)
    o_ref[...] = h_ref[...] + _dot(z_ref[...], w2_ref[...]) + b2_ref[...]


def _conv_mixer(h, gain, w_pw1, b_pw1, w_dw, b_dw, ln_g, ln_b, w_pw2, b_pw2, batch, seq, tm=512, rc=32):
    m, d = h.shape
    tm = min(tm, seq)
    nt = seq // tm
    hb = tm // CONV_HALO
    kern = functools.partial(_conv_kernel, tm=tm, rc=rc)
    return pl.pallas_call(
        kern, grid=(batch, nt),
        in_specs=[
            pl.BlockSpec((tm, d), lambda b, i: (b * nt + i, 0)),
            pl.BlockSpec((CONV_HALO, d), lambda b, i: (jnp.maximum((b * nt + i) * hb - 1, 0), 0)),
            pl.BlockSpec((1, d), lambda b, i: (0, 0)),
            pl.BlockSpec((d, 2 * d), lambda b, i: (0, 0)),
            pl.BlockSpec((1, 2 * d), lambda b, i: (0, 0)),
            pl.BlockSpec((CONV_WIDTH, d), lambda b, i: (0, 0)),
            pl.BlockSpec((1, d), lambda b, i: (0, 0)),
            pl.BlockSpec((1, d), lambda b, i: (0, 0)),
            pl.BlockSpec((1, d), lambda b, i: (0, 0)),
            pl.BlockSpec((d, d), lambda b, i: (0, 0)),
            pl.BlockSpec((1, d), lambda b, i: (0, 0)),
        ],
        out_specs=pl.BlockSpec((tm, d), lambda b, i: (b * nt + i, 0)),
        out_shape=jax.ShapeDtypeStruct((m, d), F32),
        scratch_shapes=[pltpu.VMEM((tm + CONV_HALO, d), F32), pltpu.VMEM((tm, d), BF16)],
        compiler_params=_params("parallel", "arbitrary"),
        name="conv_mixer",
    )(h, h, _row(gain), w_pw1.astype(BF16), _row(b_pw1), w_dw, _row(b_dw), _row(ln_g), _row(ln_b),
      w_pw2.astype(BF16), _row(b_pw2))


def _nsa_compress_kernel(x_ref, plo_ref, phi_ref, w1a_ref, w1b_ref, w2_ref, w2t_ref, k_ref, vt_ref):
    x = x_ref[...]
    a = _dot((x + plo_ref[...]).astype(BF16), w1a_ref[...])
    b = _dot((x + phi_ref[...]).astype(BF16), w1b_ref[...])
    b_next = jnp.concatenate([b[1:], b[:1]], axis=0)
    hid = _gelu_tanh(a + b_next).astype(BF16)
    k_ref[...] = _dot(hid, w2_ref[...]).astype(k_ref.dtype)
    vt_ref[...] = _dot_nt(w2t_ref[...], hid).astype(vt_ref.dtype)


def _nsa_compress(x2, pe, w1, w2):
    two, batch, groups, ns, kd = x2.shape
    dh = w2.shape[2]
    hid = w1.shape[2]
    plo = pe[:, :NSA_CMP_STRIDE].reshape(two, 1, kd)
    phi = pe[:, NSA_CMP_STRIDE:].reshape(two, 1, kd)
    w1a = w1[:, :kd].astype(BF16)
    w1b = w1[:, kd:].astype(BF16)
    w2b = w2.astype(BF16)
    w2t = jnp.swapaxes(w2, 1, 2).astype(BF16)
    sq = pl.Squeezed()
    return pl.pallas_call(
        _nsa_compress_kernel, grid=(two, batch, groups),
        in_specs=[
            pl.BlockSpec((sq, sq, sq, ns, kd), lambda c, b, g: (c, b, g, 0, 0)),
            pl.BlockSpec((sq, 1, kd), lambda c, b, g: (c, 0, 0)),
            pl.BlockSpec((sq, 1, kd), lambda c, b, g: (c, 0, 0)),
            pl.BlockSpec((sq, kd, hid), lambda c, b, g: (c, 0, 0)),
            pl.BlockSpec((sq, kd, hid), lambda c, b, g: (c, 0, 0)),
            pl.BlockSpec((sq, hid, dh), lambda c, b, g: (c, 0, 0)),
            pl.BlockSpec((sq, dh, hid), lambda c, b, g: (c, 0, 0)),
        ],
        out_specs=[
            pl.BlockSpec((sq, sq, sq, ns, dh), lambda c, b, g: (c, b, g, 0, 0)),
            pl.BlockSpec((sq, sq, sq, dh, ns), lambda c, b, g: (c, b, g, 0, 0)),
        ],
        out_shape=[
            jax.ShapeDtypeStruct((two, batch, groups, ns, dh), BF16),
            jax.ShapeDtypeStruct((two, batch, groups, dh, ns), BF16),
        ],
        compiler_params=_params("parallel", "parallel", "parallel"),
        name="nsa_compress",
    )(x2, plo, phi, w1a, w1b, w2b, w2t)


def _nsa_attn_kernel(sl_ref, q_ref, kc_ref, vct_ref, ks_ref, vst_ref, kw_ref, vwt_ref, gt_ref, ovt_ref,
                     o_ref, x_ref, sel_ref, *, n_sel, top_n):
    qb, dh, sel_blk = NSA_Q_BLOCK, NSA_HEAD_DIM, NSA_SEL_BLOCK
    rep = q_ref.shape[1] // dh
    g_idx = pl.program_id(1)
    i_blk = pl.program_id(2)
    q0 = i_blk * qb
    nc_pad = kc_ref.shape[0]

    q_tile = q_ref[...]
    q = jnp.concatenate([q_tile[:, r * dh:(r + 1) * dh] for r in range(rep)], axis=0)
    slopes = [sl_ref[g_idx * rep + r] for r in range(rep)]
    t_row = q0 + lax.broadcasted_iota(jnp.int32, (1, qb), 1)

    n_col = lax.broadcasted_iota(jnp.int32, (nc_pad, 1), 0)
    c_start = n_col * NSA_CMP_STRIDE
    c_center = c_start.astype(F32) + (NSA_CMP_BLOCK - 1) / 2.0
    mask_c = (c_start + (NSA_CMP_BLOCK - 1) <= t_row) & (n_col < nc_pad - 1)
    dist_c = t_row.astype(F32) - c_center
    s_all = _dot_nt(kc_ref[...], q)
    p_sum = jnp.zeros((nc_pad, qb), F32)
    p_parts = []
    for r in range(rep):
        s = s_all[:, r * qb:(r + 1) * qb] - slopes[r] * dist_c
        s = jnp.where(mask_c, s, NEG_BIG)
        m = jnp.max(s, axis=0, keepdims=True)
        e = jnp.where(mask_c, jnp.exp(s - m), 0.0)
        l = jnp.sum(e, axis=0, keepdims=True)
        p = e * jnp.where(l > 0.0, 1.0 / l, 0.0)
        p_sum = p_sum + p
        p_parts.append(p.astype(BF16))
    o_c = _dot(vct_ref[...], jnp.concatenate(p_parts, axis=1))

    p_hi, p_lo = _split_bf16(p_sum)
    imp = _dot(ovt_ref[...], p_hi) + _dot(ovt_ref[...], p_lo)
    j_col = lax.broadcasted_iota(jnp.int32, (n_sel, 1), 0)
    cur = t_row // sel_blk
    forced = (j_col == 0) | (j_col == cur) | (j_col == cur - 1)
    x = jnp.where(forced, jnp.inf, jnp.where(j_col * sel_blk <= t_row, imp, -jnp.inf))
    x_ref[...] = x
    n_grp = n_sel // 8
    xg = [x[8 * a:8 * a + 8] for a in range(n_grp)]
    cnt = [jnp.zeros((8, qb), jnp.int32) for _ in range(n_grp)]
    jj = lax.broadcasted_iota(jnp.int32, (8, 1), 0)
    for i in range(n_sel):
        xi = x_ref[i:i + 1, :]
        for a in range(n_grp):
            if 8 * a > i:
                cnt[a] = cnt[a] + jnp.where(xi >= xg[a], 1, 0)
            elif 8 * a + 7 < i:
                cnt[a] = cnt[a] + jnp.where(xi > xg[a], 1, 0)
            else:
                tie = jnp.where(jj + 8 * a > i, 1, 0)
                cnt[a] = cnt[a] + jnp.where(xi > xg[a], 1, jnp.where(xi == xg[a], tie, 0))
    for a in range(n_grp):
        sel_ref[8 * a:8 * a + 8, :] = jnp.where(cnt[a] < top_n, 1.0, 0.0)

    p_row = lax.broadcasted_iota(jnp.int32, (qb, 1), 0)
    half = p_row < sel_blk

    def attend(k_ref, vt_ref, kb_lo, kb_hi, windowed):
        def tile(kb, carry):
            m_run, l_run, acc = carry
            k0 = pl.multiple_of(kb * qb, qb)
            k_t = k_ref[pl.ds(k0, qb), :]
            v_t = vt_ref[:, pl.ds(k0, qb)]
            s_t = _dot_nt(k_t, q)
            dist = t_row - (k0 + p_row)
            if windowed:
                mask = (dist >= 0) & (dist < NSA_WINDOW)
            else:
                sel_a = sel_ref[pl.ds(2 * kb, 1), :]
                sel_b = sel_ref[pl.ds(2 * kb + 1, 1), :]
                mask = (dist >= 0) & (jnp.where(half, sel_a, sel_b) > 0.5)
            dist_f = dist.astype(F32)
            m_new, l_new, p_parts = [], [], []
            for r in range(rep):
                cols = slice(r * qb, (r + 1) * qb)
                s = jnp.where(mask, s_t[:, cols] - slopes[r] * dist_f, NEG_BIG)
                m_r = jnp.maximum(m_run[:, cols], jnp.max(s, axis=0, keepdims=True))
                e = jnp.where(mask, jnp.exp(s - m_r), 0.0)
                alpha = jnp.exp(m_run[:, cols] - m_r)
                l_new.append(alpha * l_run[:, cols] + jnp.sum(e, axis=0, keepdims=True))
                m_new.append(m_r)
                p_parts.append(e.astype(BF16))
            m_new = jnp.concatenate(m_new, axis=1)
            l_new = jnp.concatenate(l_new, axis=1)
            alpha_all = jnp.exp(m_run - m_new)
            acc = alpha_all * acc + _dot(v_t, jnp.concatenate(p_parts, axis=1))
            return m_new, l_new, acc

        init = (jnp.full((1, rep * qb), NEG_BIG, F32), jnp.zeros((1, rep * qb), F32),
                jnp.zeros((dh, rep * qb), F32))
        _, l_fin, acc = lax.fori_loop(kb_lo, kb_hi, tile, init)
        return acc * (1.0 / l_fin)

    o_s = attend(ks_ref, vst_ref, 0, i_blk + 1, False)
    o_w = attend(kw_ref, vwt_ref, jnp.maximum(i_blk - NSA_WINDOW // qb, 0), i_blk + 1, True)

    for r in range(rep):
        cols = slice(r * qb, (r + 1) * qb)
        g_c = gt_ref[3 * r:3 * r + 1, :]
        g_s = gt_ref[3 * r + 1:3 * r + 2, :]
        g_w = gt_ref[3 * r + 2:3 * r + 3, :]
        o_ref[r * dh:(r + 1) * dh, :] = (g_c * o_c[:, cols] + g_s * o_s[:, cols]
                                         + g_w * o_w[:, cols]).astype(o_ref.dtype)


def _alibi_slopes(n):
    return 2.0 ** (-8.0 * jnp.arange(1, n + 1, dtype=F32) / n)


def _nsa_overlap_t(nc_pad, n_sel):
    c_start = NSA_CMP_STRIDE * jnp.arange(nc_pad, dtype=jnp.int32)
    s_start = NSA_SEL_BLOCK * jnp.arange(n_sel, dtype=jnp.int32)
    ov = jnp.clip(jnp.minimum(c_start[:, None] + NSA_CMP_BLOCK, s_start[None] + NSA_SEL_BLOCK)
                  - jnp.maximum(c_start[:, None], s_start[None]), 0, None).astype(F32) / NSA_CMP_BLOCK
    return ov.T.astype(BF16)


def _nsa_mixer(h, gain, w_in, pe_k, w_ck1, w_ck2, pe_v, w_cv1, w_cv2, w_out, batch, seq):
    m, d = h.shape
    heads, groups, dh = NSA_HEADS, NSA_KV_HEADS, NSA_HEAD_DIM
    rep = heads // groups
    qb = NSA_Q_BLOCK
    n_q = heads * dh
    n_kv = groups * dh
    n_sel = seq // NSA_SEL_BLOCK
    nc_pad = seq // NSA_CMP_STRIDE
    top_n = min(NSA_TOP_N, n_sel)

    w_att = jnp.concatenate([w_in[:, :n_q] * dh ** -0.5, w_in[:, n_q + 2 * n_kv:n_q + 6 * n_kv]], axis=1)
    att = _linear(h, w_att, gain=gain, out_dtype=BF16)
    cmp_src = _linear(h, w_in[:, n_q:n_q + 2 * n_kv], gain=gain)
    n_gate = w_in.shape[1] - (n_q + 6 * n_kv)
    w_gate = jnp.zeros((d, LANES), F32).at[:, :n_gate].set(w_in[:, n_q + 6 * n_kv:])
    gates = _linear(h, w_gate, gain=gain, act="sigmoid", tn=LANES)[:, :n_gate]

    def slab(c):
        cols = att[:, n_q + c * n_kv:n_q + (c + 1) * n_kv]
        return cols.reshape(batch, seq, groups, dh).transpose(0, 2, 1, 3)

    k_sel, v_sel, k_win, v_win = slab(0), slab(1), slab(2), slab(3)
    v_sel_t = jnp.swapaxes(v_sel, 2, 3)
    v_win_t = jnp.swapaxes(v_win, 2, 3)
    x2 = cmp_src.reshape(batch, seq, 2, groups, dh).transpose(2, 0, 3, 1, 4)
    x2 = x2.reshape(2, batch, groups, nc_pad, NSA_CMP_STRIDE * dh)
    k_cmp_both, v_cmp_t_both = _nsa_compress(
        x2, jnp.stack([pe_k, pe_v]), jnp.stack([w_ck1, w_cv1]), jnp.stack([w_ck2, w_cv2]))
    k_cmp, v_cmp_t = k_cmp_both[0], v_cmp_t_both[1]
    gates_t = gates.reshape(batch, seq, groups, rep * 3).transpose(0, 2, 3, 1)

    sq = pl.Squeezed()
    nq = seq // qb
    kern = functools.partial(_nsa_attn_kernel, n_sel=n_sel, top_n=top_n)
    o_t = pl.pallas_call(
        kern, grid=(batch, groups, nq),
        in_specs=[
            pl.BlockSpec(memory_space=pltpu.SMEM),
            pl.BlockSpec((qb, rep * dh), lambda b, g, i: (b * nq + i, g)),
            pl.BlockSpec((sq, sq, nc_pad, dh), lambda b, g, i: (b, g, 0, 0)),
            pl.BlockSpec((sq, sq, dh, nc_pad), lambda b, g, i: (b, g, 0, 0)),
            pl.BlockSpec((sq, sq, seq, dh), lambda b, g, i: (b, g, 0, 0)),
            pl.BlockSpec((sq, sq, dh, seq), lambda b, g, i: (b, g, 0, 0)),
            pl.BlockSpec((sq, sq, seq, dh), lambda b, g, i: (b, g, 0, 0)),
            pl.BlockSpec((sq, sq, dh, seq), lambda b, g, i: (b, g, 0, 0)),
            pl.BlockSpec((sq, sq, rep * 3, qb), lambda b, g, i: (b, g, 0, i)),
            pl.BlockSpec((n_sel, nc_pad), lambda b, g, i: (0, 0)),
        ],
        out_specs=pl.BlockSpec((sq, rep * dh, qb), lambda b, g, i: (b, g, i)),
        out_shape=jax.ShapeDtypeStruct((batch, heads * dh, seq), BF16),
        scratch_shapes=[pltpu.VMEM((n_sel, qb), F32), pltpu.VMEM((n_sel, qb), F32)],
        compiler_params=_params("parallel", "parallel", "arbitrary"),
        name="nsa_attn",
    )(_alibi_slopes(heads), att, k_cmp, v_cmp_t, k_sel, v_sel_t, k_win, v_win_t, gates_t,
      _nsa_overlap_t(nc_pad, n_sel))
    o = jnp.swapaxes(o_t, 1, 2).reshape(m, heads * dh)
    return _linear(o, w_out, residual=h)


def kernel(x, norm_mix, norm_ffn, norm_final, w_ffn_gate, w_ffn_up, w_ffn_down, a_w_in, a_w_gate2, a_b_gate, a_g_norm, a_w_out, b_w_in, b_b_in, b_ln_g, b_ln_b, b_w_s, b_b_s, b_w_out, c_w_pw1, c_b_pw1, c_w_dw, c_b_dw, c_ln_g, c_ln_b, c_w_pw2, c_b_pw2, d_w_in, d_pe_k, d_w_ck1, d_w_ck2, d_pe_v, d_w_cv1, d_w_cv2, d_w_out):
    batch, seq, d = x.shape
    depth = norm_mix.shape[0]
    h = x.reshape(batch * seq, d)
    for i in range(depth):
        m, j = i % 4, i // 4
        if m == 0:
            h = _gla_mixer(h, norm_mix[i], a_w_in[j], a_w_gate2[j], a_b_gate[j], a_g_norm[j], a_w_out[j],
                           batch, seq)
        elif m == 1:
            h = _gmlp_mixer(h, norm_mix[i], b_w_in[j], b_b_in[j], b_ln_g[j], b_ln_b[j], b_w_s[j], b_b_s[j],
                            b_w_out[j])
        elif m == 2:
            h = _conv_mixer(h, norm_mix[i], c_w_pw1[j], c_b_pw1[j], c_w_dw[j], c_b_dw[j], c_ln_g[j],
                            c_ln_b[j], c_w_pw2[j], c_b_pw2[j], batch, seq)
        else:
            h = _nsa_mixer(h, norm_mix[i], d_w_in[j], d_pe_k[j], d_w_ck1[j], d_w_ck2[j], d_pe_v[j],
                           d_w_cv1[j], d_w_cv2[j], d_w_out[j], batch, seq)
        final = norm_final if i == depth - 1 else None
        h = _ffn(h, norm_ffn[i], w_ffn_gate[i], w_ffn_up[i], w_ffn_down[i], final_gain=final)
    return h.reshape(batch, seq, d)
```

```python
import functools

import jax
import jax.numpy as jnp
from jax import lax
from jax.experimental import pallas as pl
from jax.experimental.pallas import tpu as pltpu

F32 = jnp.float32
BF16 = jnp.bfloat16

EPS = 1e-6
V7X_VMEM_BYTES = 64 * 1024 * 1024
VMEM_LIMIT_BYTES = V7X_VMEM_BYTES - 8 * 1024 * 1024
LANES = 128
SUBLANES = 8

GLA_HEADS = 4
GLA_TAU = 16.0
GLA_CHUNK = 64
GLA_SUB = 16
GMLP_GROUPS = 8
GMLP_CHUNK = 128
CONV_WIDTH = 31
CONV_HALO = 32
NSA_HEADS = 16
NSA_KV_HEADS = 4
NSA_HEAD_DIM = 64
NSA_CMP_BLOCK = 32
NSA_CMP_STRIDE = 16
NSA_SEL_BLOCK = 64
NSA_TOP_N = 16
NSA_WINDOW = 512
NSA_Q_BLOCK = 128
NSA_KEY_TILE = 256
NEG_BIG = -1e30

NT_DIMS = (((1,), (1,)), ((), ()))
TN_DIMS = (((0,), (0,)), ((), ()))


def _params(*sem):
    return pltpu.CompilerParams(dimension_semantics=sem, vmem_limit_bytes=VMEM_LIMIT_BYTES)


def _dot(a, b):
    return jnp.dot(a, b, preferred_element_type=F32)


def _dot_nt(a, b):
    return lax.dot_general(a, b, NT_DIMS, preferred_element_type=F32)


def _dot_tn(a, b):
    return lax.dot_general(a, b, TN_DIMS, preferred_element_type=F32)


def _sigmoid(x):
    return 1.0 / (1.0 + jnp.exp(-x))


def _silu(x):
    return x * _sigmoid(x)


def _gelu_tanh(x):
    return 0.5 * x * (1.0 + jnp.tanh(0.7978845608028654 * (x + 0.044715 * (x * x * x))))


def _log_sigmoid(x):
    return jnp.minimum(x, 0.0) - jnp.log(1.0 + jnp.exp(-jnp.abs(x)))


def _rms(x, gain):
    ms = jnp.mean(x * x, axis=-1, keepdims=True)
    return x * lax.rsqrt(ms + EPS) * gain


def _layer_norm(x, gain, bias):
    mu = jnp.mean(x, axis=-1, keepdims=True)
    xc = x - mu
    var = jnp.mean(xc * xc, axis=-1, keepdims=True)
    return xc * lax.rsqrt(var + EPS) * gain + bias


def _split_bf16(x):
    hi = x.astype(BF16)
    lo = (x - hi.astype(F32)).astype(BF16)
    return hi, lo


def _row(v):
    return v.reshape(1, -1)


def _linear_kernel(*refs, has_gain, has_bias, has_res, act):
    refs = list(refs)
    x_ref = refs.pop(0)
    g_ref = refs.pop(0) if has_gain else None
    w_ref = refs.pop(0)
    b_ref = refs.pop(0) if has_bias else None
    r_ref = refs.pop(0) if has_res else None
    o_ref = refs.pop(0)

    if has_gain:
        xn = _rms(x_ref[...].astype(F32), g_ref[...]).astype(BF16)
    else:
        xn = x_ref[...].astype(BF16)
    y = _dot(xn, w_ref[...])
    if has_bias:
        y = y + b_ref[...]
    if act == "gelu":
        y = _gelu_tanh(y)
    elif act == "sigmoid":
        y = _sigmoid(y)
    if has_res:
        y = y + r_ref[...]
    o_ref[...] = y.astype(o_ref.dtype)


def _resident(shape):
    return pl.BlockSpec(shape, lambda *_: (0,) * len(shape), pipeline_mode=pl.Buffered(1))


def _linear(x, w, *, gain=None, bias=None, residual=None, act=None, out_dtype=F32, tm=512):
    m, k = x.shape
    n = w.shape[1]
    tm = min(tm, m)
    in_specs = [pl.BlockSpec((tm, k), lambda i: (i, 0))]
    args = [x]
    if gain is not None:
        in_specs.append(_resident((1, k)))
        args.append(_row(gain))
    in_specs.append(_resident((k, n)))
    args.append(w.astype(BF16))
    if bias is not None:
        in_specs.append(_resident((1, n)))
        args.append(_row(bias))
    if residual is not None:
        in_specs.append(pl.BlockSpec((tm, n), lambda i: (i, 0)))
        args.append(residual)
    kern = functools.partial(_linear_kernel, has_gain=gain is not None, has_bias=bias is not None,
                             has_res=residual is not None, act=act)
    return pl.pallas_call(
        kern, grid=(m // tm,), in_specs=in_specs,
        out_specs=pl.BlockSpec((tm, n), lambda i: (i, 0)),
        out_shape=jax.ShapeDtypeStruct((m, n), out_dtype),
        compiler_params=_params("parallel"),
        name="linear",
    )(*args)


def _ffn_kernel(*refs, has_final):
    refs = list(refs)
    h_ref, g_ref, wg_ref, wu_ref, wd_ref = refs[:5]
    fg_ref = refs[5] if has_final else None
    o_ref = refs[-1]
    x = h_ref[...]
    xn = _rms(x, g_ref[...]).astype(BF16)
    a = (_silu(_dot(xn, wg_ref[...])) * _dot(xn, wu_ref[...])).astype(BF16)
    y = x + _dot(a, wd_ref[...])
    if has_final:
        y = _rms(y, fg_ref[...])
    o_ref[...] = y


def _ffn(h, gain, w_gate, w_up, w_down, final_gain=None, tm=512):
    m, d = h.shape
    ff = w_gate.shape[1]
    tm = min(tm, m)
    in_specs = [
        pl.BlockSpec((tm, d), lambda i: (i, 0)),
        _resident((1, d)), _resident((d, ff)), _resident((d, ff)), _resident((ff, d)),
    ]
    args = [h, _row(gain), w_gate.astype(BF16), w_up.astype(BF16), w_down.astype(BF16)]
    if final_gain is not None:
        in_specs.append(_resident((1, d)))
        args.append(_row(final_gain))
    return pl.pallas_call(
        functools.partial(_ffn_kernel, has_final=final_gain is not None),
        grid=(m // tm,), in_specs=in_specs,
        out_specs=pl.BlockSpec((tm, d), lambda i: (i, 0)),
        out_shape=jax.ShapeDtypeStruct((m, d), F32),
        compiler_params=_params("parallel"),
        name="ffn",
    )(*args)


def _gla_decay_kernel(h_ref, g_ref, w1_ref, w2_ref, b_ref, o_ref):
    xn = _rms(h_ref[...], g_ref[...]).astype(BF16)
    lr = _dot(xn, w1_ref[...]).astype(BF16)
    x = _dot(lr, w2_ref[...]) + b_ref[...]
    o_ref[...] = _log_sigmoid(x) * (1.0 / GLA_TAU)


def _gla_decay(h, gain, w_lr, w_gate2, b_gate, tm=512):
    m, d = h.shape
    rank = w_lr.shape[1]
    n = w_gate2.shape[1]
    tm = min(tm, m)
    w1 = jnp.zeros((d, LANES), BF16).at[:, :rank].set(w_lr.astype(BF16))
    w2 = jnp.zeros((LANES, n), BF16).at[:rank, :].set(w_gate2.astype(BF16))
    return pl.pallas_call(
        _gla_decay_kernel, grid=(m // tm,),
        in_specs=[
            pl.BlockSpec((tm, d), lambda i: (i, 0)),
            pl.BlockSpec((1, d), lambda i: (0, 0)),
            pl.BlockSpec((d, LANES), lambda i: (0, 0)),
            pl.BlockSpec((LANES, n), lambda i: (0, 0)),
            pl.BlockSpec((1, n), lambda i: (0, 0)),
        ],
        out_specs=pl.BlockSpec((tm, n), lambda i: (i, 0)),
        out_shape=jax.ShapeDtypeStruct((m, n), F32),
        compiler_params=_params("parallel"),
        name="gla_decay",
    )(h, _row(gain), w1, w2, _row(b_gate))


def _gla_core_kernel(q_ref, k_ref, v_ref, go_ref, la_ref, gn_ref, o_ref, st_ref, *, n_chunks, scale):
    c_len, sub = GLA_CHUNK, GLA_SUB
    n_sub = c_len // sub
    tr = n_chunks * c_len

    @pl.when(pl.program_id(2) == 0)
    def _():
        st_ref[...] = jnp.zeros_like(st_ref)

    r_io = lax.broadcasted_iota(jnp.int32, (tr, tr), 0)
    c_io = lax.broadcasted_iota(jnp.int32, (tr, tr), 1)
    tril = jnp.where((r_io >= c_io) & (r_io // c_len == c_io // c_len), 1.0, 0.0).astype(BF16)
    hi, lo = _split_bf16(la_ref[...])
    bcum = _dot(tril, hi) + _dot(tril, lo)
    q = q_ref[...] * scale
    k = k_ref[...]
    v = v_ref[...].astype(BF16)
    dk = q.shape[1]
    qe = (q * jnp.exp(bcum)).astype(BF16)

    causal = [lax.broadcasted_iota(jnp.int32, (sub, (i + 1) * sub), 1)
              <= lax.broadcasted_iota(jnp.int32, (sub, (i + 1) * sub), 0) + i * sub for i in range(n_sub)]
    qs, ks, kdecs, blasts = [], [], [], []
    for c in range(n_chunks):
        bc = bcum[c * c_len:(c + 1) * c_len]
        qc = q[c * c_len:(c + 1) * c_len]
        kc = k[c * c_len:(c + 1) * c_len]
        for i in range(n_sub):
            r0, r1 = i * sub, (i + 1) * sub
            base = bc[r0 - 1:r0, :] if i else jnp.zeros((1, dk), F32)
            qs.append((qc[r0:r1] * jnp.exp(bc[r0:r1] - base)).astype(BF16))
            ks.append((kc[:r1] * jnp.exp(base - bc[:r1])).astype(BF16))
        blast = bc[c_len - 1:c_len, :]
        blasts.append(blast)
        kdecs.append((kc * jnp.exp(blast - bc)).astype(BF16))
    scores = [_dot_nt(qi, ki) for qi, ki in zip(qs, ks)]
    scores = [jnp.where(causal[n % n_sub], a, 0.0).astype(BF16) for n, a in enumerate(scores)]
    intra = [_dot(a, v[(n // n_sub) * c_len:(n // n_sub) * c_len + (n % n_sub + 1) * sub])
             for n, a in enumerate(scores)]
    updates = [_dot_tn(v[c * c_len:(c + 1) * c_len], kdecs[c]) for c in range(n_chunks)]
    st = st_ref[...]
    inter = []
    for c in range(n_chunks):
        inter.append(_dot_nt(qe[c * c_len:(c + 1) * c_len], st.astype(BF16)))
        st = st * jnp.exp(blasts[c]) + updates[c]
    st_ref[...] = st
    o = jnp.concatenate(inter, axis=0) + jnp.concatenate(intra, axis=0)
    o_ref[...] = (_rms(o, gn_ref[...]) * _silu(go_ref[...])).astype(o_ref.dtype)


def _gla_core(proj, log_a, g_norm, batch, seq, tr=512):
    t = proj.shape[0]
    h_n = GLA_HEADS
    dk = log_a.shape[1] // h_n
    dv = (proj.shape[1] - 2 * h_n * dk) // (2 * h_n)
    tr = min(tr, seq)
    nt = seq // tr
    kb, vb, gb = h_n, (2 * h_n * dk) // dv, (2 * h_n * dk) // dv + h_n
    kern = functools.partial(_gla_core_kernel, n_chunks=tr // GLA_CHUNK, scale=dk ** -0.5)
    return pl.pallas_call(
        kern, grid=(batch, h_n, nt),
        in_specs=[
            pl.BlockSpec((tr, dk), lambda b, h, i: (b * nt + i, h)),
            pl.BlockSpec((tr, dk), lambda b, h, i: (b * nt + i, kb + h)),
            pl.BlockSpec((tr, dv), lambda b, h, i: (b * nt + i, vb + h)),
            pl.BlockSpec((tr, dv), lambda b, h, i: (b * nt + i, gb + h)),
            pl.BlockSpec((tr, dk), lambda b, h, i: (b * nt + i, h)),
            pl.BlockSpec((1, dv), lambda b, h, i: (0, 0)),
        ],
        out_specs=pl.BlockSpec((tr, dv), lambda b, h, i: (b * nt + i, h)),
        out_shape=jax.ShapeDtypeStruct((t, h_n * dv), BF16),
        scratch_shapes=[pltpu.VMEM((dv, dk), F32)],
        compiler_params=_params("parallel", "parallel", "arbitrary"),
        name="gla_core",
    )(proj, proj, proj, proj, log_a, _row(g_norm))


def _gla_mixer(h, gain, w_in, w_gate2, b_gate, g_norm, w_out, batch, seq):
    n_main = w_in.shape[1] - w_gate2.shape[0]
    proj = _linear(h, w_in[:, :n_main], gain=gain)
    log_a = _gla_decay(h, gain, w_in[:, n_main:], w_gate2, b_gate)
    o = _gla_core(proj, log_a, g_norm, batch, seq)
    return _linear(o, w_out, residual=h)


def _gmlp_sgu_kernel(u_ref, v_ref, lg_ref, lb_ref, ws_ref, bs_ref, wo_ref, h_ref, o_ref, p_ref, *, n_chunks):
    c_len = GMLP_CHUNK
    groups = ws_ref.shape[0]
    e = v_ref.shape[1]
    ge = e // groups
    causal = (lax.broadcasted_iota(jnp.int32, (c_len, c_len), 0)
              >= lax.broadcasted_iota(jnp.int32, (c_len, c_len), 1))
    for c in range(n_chunks):
        rows = slice(c * c_len, (c + 1) * c_len)
        vn = _layer_norm(v_ref[rows, :].astype(F32), lg_ref[...], lb_ref[...]).astype(BF16)
        for g in range(groups):
            cols = slice(g * ge, (g + 1) * ge)
            ws = jnp.where(causal, ws_ref[g], 0.0).astype(BF16)
            sv = _dot(ws, vn[:, cols]) + bs_ref[:, cols]
            p_ref[rows, cols] = (u_ref[rows, cols].astype(F32) * sv).astype(BF16)
    o_ref[...] = h_ref[...] + _dot(p_ref[...], wo_ref[...])


def _gmlp_mixer(h, gain, w_in, b_in, ln_g, ln_b, w_s, b_s, w_out, tm=256):
    m, d = h.shape
    e = w_out.shape[0]
    groups, c_len = w_s.shape[0], w_s.shape[1]
    z = _linear(h, w_in, gain=gain, bias=b_in, act="gelu", out_dtype=BF16)
    tm = min(tm, m)
    bs_full = jnp.repeat(b_s.T, e // groups, axis=1)
    return pl.pallas_call(
        functools.partial(_gmlp_sgu_kernel, n_chunks=tm // c_len),
        grid=(m // tm,),
        in_specs=[
            pl.BlockSpec((tm, e), lambda i: (i, 0)),
            pl.BlockSpec((tm, e), lambda i: (i, 1)),
            pl.BlockSpec((1, e), lambda i: (0, 0)),
            pl.BlockSpec((1, e), lambda i: (0, 0)),
            pl.BlockSpec((groups, c_len, c_len), lambda i: (0, 0, 0)),
            pl.BlockSpec((c_len, e), lambda i: (0, 0)),
            pl.BlockSpec((e, d), lambda i: (0, 0)),
            pl.BlockSpec((tm, d), lambda i: (i, 0)),
        ],
        out_specs=pl.BlockSpec((tm, d), lambda i: (i, 0)),
        out_shape=jax.ShapeDtypeStruct((m, d), F32),
        scratch_shapes=[pltpu.VMEM((tm, e), BF16)],
        compiler_params=_params("parallel"),
        name="gmlp_sgu",
    )(z, z, _row(ln_g), _row(ln_b), w_s, bs_full, w_out.astype(BF16), h)


def _conv_kernel(h_ref, halo_ref, g_ref, w1_ref, b1_ref, wdw_ref, bdw_ref, lg_ref, lb_ref, w2_ref, b2_ref,
                 o_ref, y_ref, c_ref, *, tm, rc, lc):
    d = h_ref.shape[1]
    halo = CONV_HALO

    def glu(x):
        ag = _dot(_rms(x, g_ref[...]).astype(BF16), w1_ref[...]) + b1_ref[...]
        return ag[:, :d] * _sigmoid(ag[:, d:])

    first = pl.program_id(1) == 0
    y_ref[0:halo, :] = jnp.where(first, 0.0, glu(halo_ref[...]))
    y_ref[halo:, :] = glu(h_ref[...])
    off = halo - (CONV_WIDTH - 1)
    sl = SUBLANES
    for r in range(tm // rc):
        for lb in range(d // lc):
            lanes = slice(lb * lc, (lb + 1) * lc)
            out = None
            for phase in range(sl):
                rows = rc + (sl if phase else 0)
                part = None
                for j in range(CONV_WIDTH):
                    if (off + j) % sl != phase:
                        continue
                    a = r * rc + (off + j) // sl * sl
                    win = y_ref[a:a + rows, lanes].reshape(rows // sl, sl, lc)
                    term = win * wdw_ref[sl * j:sl * (j + 1), lanes][None]
                    part = term if part is None else part + term
                part = part.reshape(rows, lc)[phase:phase + rc]
                out = part if out is None else out + part
            c_ref[r * rc:(r + 1) * rc, lanes] = out
    yc = _layer_norm(c_ref[...] + bdw_ref[...], lg_ref[...], lb_ref[...])
    o_ref[...] = h_ref[...] + _dot(_silu(yc).astype(BF16), w2_ref[...]) + b2_ref[...]


def _conv_mixer(h, gain, w_pw1, b_pw1, w_dw, b_dw, ln_g, ln_b, w_pw2, b_pw2, batch, seq, tm=512, rc=32, lc=256):
    m, d = h.shape
    tm = min(tm, seq)
    nt = seq // tm
    hb = tm // CONV_HALO
    kern = functools.partial(_conv_kernel, tm=tm, rc=rc, lc=lc)
    return pl.pallas_call(
        kern, grid=(batch, nt),
        in_specs=[
            pl.BlockSpec((tm, d), lambda b, i: (b * nt + i, 0)),
            pl.BlockSpec((CONV_HALO, d), lambda b, i: (jnp.maximum((b * nt + i) * hb - 1, 0), 0)),
            pl.BlockSpec((1, d), lambda b, i: (0, 0)),
            pl.BlockSpec((d, 2 * d), lambda b, i: (0, 0)),
            pl.BlockSpec((1, 2 * d), lambda b, i: (0, 0)),
            pl.BlockSpec((CONV_WIDTH * SUBLANES, d), lambda b, i: (0, 0)),
            pl.BlockSpec((1, d), lambda b, i: (0, 0)),
            pl.BlockSpec((1, d), lambda b, i: (0, 0)),
            pl.BlockSpec((1, d), lambda b, i: (0, 0)),
            pl.BlockSpec((d, d), lambda b, i: (0, 0)),
            pl.BlockSpec((1, d), lambda b, i: (0, 0)),
        ],
        out_specs=pl.BlockSpec((tm, d), lambda b, i: (b * nt + i, 0)),
        out_shape=jax.ShapeDtypeStruct((m, d), F32),
        scratch_shapes=[pltpu.VMEM((tm + CONV_HALO, d), F32), pltpu.VMEM((tm, d), F32)],
        compiler_params=_params("parallel", "arbitrary"),
        name="conv_mixer",
    )(h, h, _row(gain), w_pw1.astype(BF16), _row(b_pw1), jnp.repeat(w_dw, SUBLANES, axis=0),
      _row(b_dw), _row(ln_g), _row(ln_b),
      w_pw2.astype(BF16), _row(b_pw2))


def _nsa_compress_kernel(x_ref, plo_ref, phi_ref, w1a_ref, w1b_ref, w2_ref, w2t_ref, k_ref, vt_ref):
    x = x_ref[...]
    a = _dot((x + plo_ref[...]).astype(BF16), w1a_ref[...])
    b = _dot((x + phi_ref[...]).astype(BF16), w1b_ref[...])
    b_next = jnp.concatenate([b[1:], b[:1]], axis=0)
    hid = _gelu_tanh(a + b_next).astype(BF16)
    k_ref[...] = _dot(hid, w2_ref[...]).astype(k_ref.dtype)
    vt_ref[...] = _dot_nt(w2t_ref[...], hid).astype(vt_ref.dtype)


def _nsa_compress(x2, pe, w1, w2):
    two, batch, groups, ns, kd = x2.shape
    dh = w2.shape[2]
    hid = w1.shape[2]
    plo = pe[:, :NSA_CMP_STRIDE].reshape(two, 1, kd)
    phi = pe[:, NSA_CMP_STRIDE:].reshape(two, 1, kd)
    w1a = w1[:, :kd].astype(BF16)
    w1b = w1[:, kd:].astype(BF16)
    w2b = w2.astype(BF16)
    w2t = jnp.swapaxes(w2, 1, 2).astype(BF16)
    sq = pl.Squeezed()
    return pl.pallas_call(
        _nsa_compress_kernel, grid=(two, batch, groups),
        in_specs=[
            pl.BlockSpec((sq, sq, sq, ns, kd), lambda c, b, g: (c, b, g, 0, 0)),
            pl.BlockSpec((sq, 1, kd), lambda c, b, g: (c, 0, 0)),
            pl.BlockSpec((sq, 1, kd), lambda c, b, g: (c, 0, 0)),
            pl.BlockSpec((sq, kd, hid), lambda c, b, g: (c, 0, 0)),
            pl.BlockSpec((sq, kd, hid), lambda c, b, g: (c, 0, 0)),
            pl.BlockSpec((sq, hid, dh), lambda c, b, g: (c, 0, 0)),
            pl.BlockSpec((sq, dh, hid), lambda c, b, g: (c, 0, 0)),
        ],
        out_specs=[
            pl.BlockSpec((sq, sq, sq, ns, dh), lambda c, b, g: (c, b, g, 0, 0)),
            pl.BlockSpec((sq, sq, sq, dh, ns), lambda c, b, g: (c, b, g, 0, 0)),
        ],
        out_shape=[
            jax.ShapeDtypeStruct((two, batch, groups, ns, dh), BF16),
            jax.ShapeDtypeStruct((two, batch, groups, dh, ns), BF16),
        ],
        compiler_params=_params("parallel", "parallel", "parallel"),
        name="nsa_compress",
    )(x2, plo, phi, w1a, w1b, w2b, w2t)


def _nsa_select_kernel(sl_ref, q_ref, kc_ref, vct_ref, ovt_ref, pool_ref, oc_ref, selb_ref, flag_ref, x_ref,
                       *, n_sel, top_n):
    qb, dh, sel_blk = NSA_Q_BLOCK, NSA_HEAD_DIM, NSA_SEL_BLOCK
    rep = q_ref.shape[1] // dh
    g_idx = pl.program_id(1)
    i_blk = pl.program_id(2)
    q0 = i_blk * qb
    nc_pad = kc_ref.shape[0]

    q_tile = q_ref[...]
    q = jnp.concatenate([q_tile[:, r * dh:(r + 1) * dh] for r in range(rep)], axis=0)
    slopes = [sl_ref[g_idx * rep + r] for r in range(rep)]
    t_row = q0 + lax.broadcasted_iota(jnp.int32, (1, qb), 1)

    n_col = lax.broadcasted_iota(jnp.int32, (nc_pad, 1), 0)
    c_start = n_col * NSA_CMP_STRIDE
    c_center = c_start.astype(F32) + (NSA_CMP_BLOCK - 1) / 2.0
    mask_c = (c_start + (NSA_CMP_BLOCK - 1) <= t_row) & (n_col < nc_pad - 1)
    dist_c = t_row.astype(F32) - c_center
    s_all = _dot_nt(kc_ref[...], q)
    p_sum = jnp.zeros((nc_pad, qb), F32)
    p_parts = []
    for r in range(rep):
        s = s_all[:, r * qb:(r + 1) * qb] - slopes[r] * dist_c
        s = jnp.where(mask_c, s, NEG_BIG)
        m = jnp.max(s, axis=0, keepdims=True)
        e = jnp.where(mask_c, jnp.exp(s - m), 0.0)
        l = jnp.sum(e, axis=0, keepdims=True)
        p = e * jnp.where(l > 0.0, 1.0 / l, 0.0)
        p_sum = p_sum + p
        p_parts.append(p.astype(BF16))
    o_c = _dot(vct_ref[...], jnp.concatenate(p_parts, axis=1))

    p_hi, p_lo = _split_bf16(p_sum)
    imp = _dot(ovt_ref[...], p_hi) + _dot(ovt_ref[...], p_lo)
    j_col = lax.broadcasted_iota(jnp.int32, (n_sel, 1), 0)
    cur = t_row // sel_blk
    forced = (j_col == 0) | (j_col == cur) | (j_col == cur - 1)
    x = jnp.where(forced, jnp.inf, jnp.where(j_col * sel_blk <= t_row, imp, -jnp.inf))
    x_ref[...] = x
    n_grp = n_sel // 8
    xg = [x[8 * a:8 * a + 8] for a in range(n_grp)]
    cnt = [jnp.zeros((8, qb), jnp.int32) for _ in range(n_grp)]
    jj = lax.broadcasted_iota(jnp.int32, (8, 1), 0)
    for i in range(n_sel):
        xi = x_ref[i:i + 1, :]
        for a in range(n_grp):
            if 8 * a > i:
                cnt[a] = cnt[a] + jnp.where(xi >= xg[a], 1, 0)
            elif 8 * a + 7 < i:
                cnt[a] = cnt[a] + jnp.where(xi > xg[a], 1, 0)
            else:
                tie = jnp.where(jj + 8 * a > i, 1, 0)
                cnt[a] = cnt[a] + jnp.where(xi > xg[a], 1, jnp.where(xi == xg[a], tie, 0))
    sel01 = []
    for a in range(n_grp):
        chosen = cnt[a] < top_n
        selb_ref[8 * a:8 * a + 8, :] = jnp.where(chosen, 0.0, NEG_BIG)
        sel01.append(jnp.where(chosen, 1.0, 0.0).astype(BF16))
    per_block = _dot_nt(jnp.ones((8, qb), BF16), jnp.concatenate(sel01, axis=0))
    flag_ref[...] = _dot(per_block.astype(BF16), pool_ref[...])
    for r in range(rep):
        oc_ref[r * dh:(r + 1) * dh, :] = o_c[:, r * qb:(r + 1) * qb]


def _nsa_attend_kernel(sl_ref, flag_ref, q_ref, qa_ref, ks_ref, vst_ref, kw_ref, vwt_ref, selb_ref, oc_ref,
                       gt_ref, o_ref, m_ref, l_ref, acc_ref):
    qb, dh, kt, sub = NSA_Q_BLOCK, NSA_HEAD_DIM, NSA_KEY_TILE, NSA_SEL_BLOCK
    nb = kt // sub
    rep = q_ref.shape[1] // dh
    g_idx = pl.program_id(1)
    q0 = pl.program_id(2) * qb
    kd = q0 // kt

    q_tile = q_ref[...]
    qp = jnp.concatenate(
        [jnp.concatenate([q_tile[:, r * dh:(r + 1) * dh], jnp.broadcast_to(qa_ref[r:r + 1, :], (qb, dh))], axis=1)
         for r in range(rep)], axis=0)
    slopes = [sl_ref[g_idx * rep + r] for r in range(rep)]
    d_rel = lax.broadcasted_iota(jnp.int32, (kt, qb), 1) - lax.broadcasted_iota(jnp.int32, (kt, qb), 0)

    def reset():
        m_ref[...] = jnp.full(m_ref.shape, NEG_BIG, F32)
        l_ref[...] = jnp.zeros(l_ref.shape, F32)
        acc_ref[...] = jnp.zeros(acc_ref.shape, F32)

    def tiles(k_ref, vt_ref, specs):
        k0s = [pl.multiple_of(kb * kt, kt) for kb, _ in specs]
        raws = [_dot_nt(k_ref[pl.ds(k0, kt), :], qp) for k0 in k0s]
        offs = [(q0 - k0).astype(F32) for k0 in k0s]
        p_parts = [[] for _ in specs]
        alphas = []
        for r in range(rep):
            cols = slice(r * qb, (r + 1) * qb)
            cs = [-slopes[r] * off for off in offs]
            ss = [jnp.concatenate([raw[b * sub:(b + 1) * sub, cols] + bias_fn(b) for b in range(nb)], axis=0)
                  for raw, (_, bias_fn) in zip(raws, specs)]
            m_old = m_ref[:, cols]
            m_new = m_old
            for s, c in zip(ss, cs):
                m_new = jnp.maximum(m_new, jnp.max(s, axis=0, keepdims=True) + c)
            alpha = jnp.exp(m_old - m_new)
            l_new = alpha * l_ref[:, cols]
            for j, (s, c) in enumerate(zip(ss, cs)):
                e = jnp.exp(s - (m_new - c))
                l_new = l_new + jnp.sum(e, axis=0, keepdims=True)
                p_parts[j].append(e.astype(BF16))
            l_ref[:, cols] = l_new
            m_ref[:, cols] = m_new
            alphas.append(alpha)
        acc = jnp.concatenate(alphas, axis=1) * acc_ref[...]
        for j, k0 in enumerate(k0s):
            acc = acc + _dot(vt_ref[:, pl.ds(k0, kt)], jnp.concatenate(p_parts[j], axis=1))
        acc_ref[...] = acc

    def finish():
        return acc_ref[...] * (1.0 / l_ref[...])

    def causal_bias(kb):
        return jnp.where(d_rel + (q0 - kb * kt) >= 0, 0.0, NEG_BIG)

    def sel_spec(kb, causal):
        if causal is None:
            return kb, lambda b: selb_ref[pl.ds(kb * nb + b, 1), :]
        return kb, lambda b: selb_ref[pl.ds(kb * nb + b, 1), :] + causal[b * sub:(b + 1) * sub]

    near = jnp.where(kd > 0, kd - 1, 1)
    reset()
    tiles(ks_ref, vst_ref, [sel_spec(kd, causal_bias(kd)), sel_spec(near, causal_bias(near))])

    def sel_body(kb, carry):
        @pl.when(flag_ref[0, kb] > 0)
        def _():
            tiles(ks_ref, vst_ref, [sel_spec(kb, None)])
        return carry

    lax.fori_loop(0, kd - 1, sel_body, 0)
    o_s = finish()

    n_win = -(-NSA_WINDOW // kt) + 1
    first = jnp.clip(kd - (n_win - 1), 0, ks_ref.shape[0] // kt - n_win)

    def win_spec(kb):
        dist = d_rel + (q0 - kb * kt)
        wb = jnp.where((dist >= 0) & (dist < NSA_WINDOW), 0.0, NEG_BIG)
        return kb, lambda b: wb[b * sub:(b + 1) * sub]

    reset()
    tiles(kw_ref, vwt_ref, [win_spec(first + j) for j in reversed(range(n_win))])
    o_w = finish()

    for r in range(rep):
        cols = slice(r * qb, (r + 1) * qb)
        rows = slice(r * dh, (r + 1) * dh)
        g_c = gt_ref[3 * r:3 * r + 1, :]
        g_s = gt_ref[3 * r + 1:3 * r + 2, :]
        g_w = gt_ref[3 * r + 2:3 * r + 3, :]
        o_ref[rows, :] = (g_c * oc_ref[rows, :] + g_s * o_s[:, cols] + g_w * o_w[:, cols]).astype(o_ref.dtype)


def _alibi_slopes(n):
    return 2.0 ** (-8.0 * jnp.arange(1, n + 1, dtype=F32) / n)


def _nsa_overlap_t(nc_pad, n_sel):
    c_start = NSA_CMP_STRIDE * jnp.arange(nc_pad, dtype=jnp.int32)
    s_start = NSA_SEL_BLOCK * jnp.arange(n_sel, dtype=jnp.int32)
    ov = jnp.clip(jnp.minimum(c_start[:, None] + NSA_CMP_BLOCK, s_start[None] + NSA_SEL_BLOCK)
                  - jnp.maximum(c_start[:, None], s_start[None]), 0, None).astype(F32) / NSA_CMP_BLOCK
    return ov.T.astype(BF16)


def _nsa_mixer(h, gain, w_in, pe_k, w_ck1, w_ck2, pe_v, w_cv1, w_cv2, w_out, batch, seq):
    m, d = h.shape
    heads, groups, dh = NSA_HEADS, NSA_KV_HEADS, NSA_HEAD_DIM
    rep = heads // groups
    qb = NSA_Q_BLOCK
    n_q = heads * dh
    n_kv = groups * dh
    n_sel = seq // NSA_SEL_BLOCK
    nc_pad = seq // NSA_CMP_STRIDE
    top_n = min(NSA_TOP_N, n_sel)

    w_att = jnp.concatenate([w_in[:, :n_q] * dh ** -0.5, w_in[:, n_q + 2 * n_kv:n_q + 6 * n_kv]], axis=1)
    att = _linear(h, w_att, gain=gain, out_dtype=BF16)
    cmp_src = _linear(h, w_in[:, n_q:n_q + 2 * n_kv], gain=gain)
    n_gate = w_in.shape[1] - (n_q + 6 * n_kv)
    w_gate = jnp.zeros((d, LANES), F32).at[:, :n_gate].set(w_in[:, n_q + 6 * n_kv:])
    gates = _linear(h, w_gate, gain=gain, act="sigmoid")[:, :n_gate]

    def slab(c):
        cols = att[:, n_q + c * n_kv:n_q + (c + 1) * n_kv]
        return cols.reshape(batch, seq, groups, dh).transpose(0, 2, 1, 3)

    k_sel, v_sel, k_win, v_win = slab(0), slab(1), slab(2), slab(3)
    v_sel_t = jnp.swapaxes(v_sel, 2, 3)
    v_win_t = jnp.swapaxes(v_win, 2, 3)
    x2 = cmp_src.reshape(batch, seq, 2, groups, dh).transpose(2, 0, 3, 1, 4)
    x2 = x2.reshape(2, batch, groups, nc_pad, NSA_CMP_STRIDE * dh)
    k_cmp_both, v_cmp_t_both = _nsa_compress(
        x2, jnp.stack([pe_k, pe_v]), jnp.stack([w_ck1, w_cv1]), jnp.stack([w_ck2, w_cv2]))
    k_cmp, v_cmp_t = k_cmp_both[0], v_cmp_t_both[1]
    gates_t = gates.reshape(batch, seq, groups, rep * 3).transpose(0, 2, 3, 1)

    sq = pl.Squeezed()
    nq = seq // qb
    kt = NSA_KEY_TILE
    n_tiles = seq // kt
    slopes = _alibi_slopes(heads)
    pool = (jnp.arange(n_sel)[:, None] // (kt // NSA_SEL_BLOCK) == jnp.arange(LANES)[None, :]).astype(BF16)
    o_c, sel_bias, tile_hits = pl.pallas_call(
        functools.partial(_nsa_select_kernel, n_sel=n_sel, top_n=top_n),
        grid=(batch, groups, nq),
        in_specs=[
            pl.BlockSpec(memory_space=pltpu.SMEM),
            pl.BlockSpec((qb, rep * dh), lambda b, g, i: (b * nq + i, g)),
            pl.BlockSpec((sq, sq, nc_pad, dh), lambda b, g, i: (b, g, 0, 0)),
            pl.BlockSpec((sq, sq, dh, nc_pad), lambda b, g, i: (b, g, 0, 0)),
            pl.BlockSpec((n_sel, nc_pad), lambda b, g, i: (0, 0)),
            pl.BlockSpec((n_sel, LANES), lambda b, g, i: (0, 0)),
        ],
        out_specs=[
            pl.BlockSpec((sq, rep * dh, qb), lambda b, g, i: (b, g, i)),
            pl.BlockSpec((sq, sq, n_sel, qb), lambda b, g, i: (b, g, 0, i)),
            pl.BlockSpec((sq, sq, sq, 8, LANES), lambda b, g, i: (b, g, i, 0, 0)),
        ],
        out_shape=[
            jax.ShapeDtypeStruct((batch, heads * dh, seq), F32),
            jax.ShapeDtypeStruct((batch, groups, n_sel, seq), F32),
            jax.ShapeDtypeStruct((batch, groups, nq, 8, LANES), F32),
        ],
        scratch_shapes=[pltpu.VMEM((n_sel, qb), F32)],
        compiler_params=_params("parallel", "parallel", "parallel"),
        name="nsa_select",
    )(slopes, att, k_cmp, v_cmp_t, _nsa_overlap_t(nc_pad, n_sel), pool)
    tile_flags = (tile_hits[:, :, :, 0, :n_tiles] > 0.5).astype(jnp.int32).reshape(batch * groups * nq, 1, n_tiles)

    pos_feat = jnp.zeros((seq, dh), BF16).at[:, :3].set((jnp.arange(seq) % kt).astype(BF16)[:, None])
    pos_feat = jnp.broadcast_to(pos_feat, (batch, groups, seq, dh))
    k_sel_aug = jnp.concatenate([k_sel, pos_feat], axis=-1)
    k_win_aug = jnp.concatenate([k_win, pos_feat], axis=-1)
    s_hi = slopes.astype(BF16)
    s_mid = (slopes - s_hi.astype(F32)).astype(BF16)
    s_lo = (slopes - s_hi.astype(F32) - s_mid.astype(F32)).astype(BF16)
    q_aug = jnp.zeros((heads, dh), BF16).at[:, 0].set(s_hi).at[:, 1].set(s_mid).at[:, 2].set(s_lo)
    q_aug = q_aug.reshape(groups, rep, dh)

    o_t = pl.pallas_call(
        _nsa_attend_kernel, grid=(batch, groups, nq),
        in_specs=[
            pl.BlockSpec(memory_space=pltpu.SMEM),
            pl.BlockSpec((sq, 1, n_tiles), lambda b, g, i: ((b * groups + g) * nq + i, 0, 0),
                         memory_space=pltpu.SMEM),
            pl.BlockSpec((qb, rep * dh), lambda b, g, i: (b * nq + i, g)),
            pl.BlockSpec((sq, rep, dh), lambda b, g, i: (g, 0, 0)),
            pl.BlockSpec((sq, sq, seq, 2 * dh), lambda b, g, i: (b, g, 0, 0)),
            pl.BlockSpec((sq, sq, dh, seq), lambda b, g, i: (b, g, 0, 0)),
            pl.BlockSpec((sq, sq, seq, 2 * dh), lambda b, g, i: (b, g, 0, 0)),
            pl.BlockSpec((sq, sq, dh, seq), lambda b, g, i: (b, g, 0, 0)),
            pl.BlockSpec((sq, sq, n_sel, qb), lambda b, g, i: (b, g, 0, i)),
            pl.BlockSpec((sq, rep * dh, qb), lambda b, g, i: (b, g, i)),
            pl.BlockSpec((sq, sq, rep * 3, qb), lambda b, g, i: (b, g, 0, i)),
        ],
        out_specs=pl.BlockSpec((sq, rep * dh, qb), lambda b, g, i: (b, g, i)),
        out_shape=jax.ShapeDtypeStruct((batch, heads * dh, seq), BF16),
        scratch_shapes=[pltpu.VMEM((1, rep * qb), F32), pltpu.VMEM((1, rep * qb), F32),
                        pltpu.VMEM((dh, rep * qb), F32)],
        compiler_params=_params("parallel", "parallel", "arbitrary"),
        name="nsa_attend",
    )(slopes, tile_flags, att, q_aug, k_sel_aug, v_sel_t, k_win_aug, v_win_t, sel_bias, o_c, gates_t)
    o = jnp.swapaxes(o_t, 1, 2).reshape(m, heads * dh)
    return _linear(o, w_out, residual=h)


def kernel(x, norm_mix, norm_ffn, norm_final, w_ffn_gate, w_ffn_up, w_ffn_down, a_w_in, a_w_gate2, a_b_gate, a_g_norm, a_w_out, b_w_in, b_b_in, b_ln_g, b_ln_b, b_w_s, b_b_s, b_w_out, c_w_pw1, c_b_pw1, c_w_dw, c_b_dw, c_ln_g, c_ln_b, c_w_pw2, c_b_pw2, d_w_in, d_pe_k, d_w_ck1, d_w_ck2, d_pe_v, d_w_cv1, d_w_cv2, d_w_out):
    batch, seq, d = x.shape
    depth = norm_mix.shape[0]
    h = x.reshape(batch * seq, d)
    for i in range(depth):
        m, j = i % 4, i // 4
        if m == 0:
            h = _gla_mixer(h, norm_mix[i], a_w_in[j], a_w_gate2[j], a_b_gate[j], a_g_norm[j], a_w_out[j],
                           batch, seq)
        elif m == 1:
            h = _gmlp_mixer(h, norm_mix[i], b_w_in[j], b_b_in[j], b_ln_g[j], b_ln_b[j], b_w_s[j], b_b_s[j],
                            b_w_out[j])
        elif m == 2:
            h = _conv_mixer(h, norm_mix[i], c_w_pw1[j], c_b_pw1[j], c_w_dw[j], c_b_dw[j], c_ln_g[j],
                            c_ln_b[j], c_w_pw2[j], c_b_pw2[j], batch, seq)
        else:
            h = _nsa_mixer(h, norm_mix[i], d_w_in[j], d_pe_k[j], d_w_ck1[j], d_w_ck2[j], d_pe_v[j],
                           d_w_cv1[j], d_w_cv2[j], d_w_out[j], batch, seq)
        final = norm_final if i == depth - 1 else None
        h = _ffn(h, norm_ffn[i], w_ffn_gate[i], w_ffn_up[i], w_ffn_down[i], final_gain=final)
    return h.reshape(batch, seq, d)
```

```python
import functools

import jax
import jax.numpy as jnp
from jax import lax
from jax.experimental import pallas as pl
from jax.experimental.pallas import tpu as pltpu

F32 = jnp.float32
BF16 = jnp.bfloat16

EPS = 1e-6
V7X_VMEM_BYTES = 64 * 1024 * 1024
VMEM_LIMIT_BYTES = V7X_VMEM_BYTES - 8 * 1024 * 1024
LANES = 128
SUBLANES = 8

GLA_HEADS = 4
GLA_TAU = 16.0
GLA_CHUNK = 64
GLA_SUB = 16
GMLP_GROUPS = 8
GMLP_CHUNK = 128
CONV_WIDTH = 31
CONV_HALO = 32
NSA_HEADS = 16
NSA_KV_HEADS = 4
NSA_HEAD_DIM = 64
NSA_CMP_BLOCK = 32
NSA_CMP_STRIDE = 16
NSA_SEL_BLOCK = 64
NSA_TOP_N = 16
NSA_WINDOW = 512
NSA_Q_BLOCK = 128
NSA_KEY_TILE = 256
NSA_CMP_CHUNK = 128
NSA_RANK_CHUNK = 32
NSA_RANK_ROWS = 64
NEG_BIG = -1e30

NT_DIMS = (((1,), (1,)), ((), ()))
TN_DIMS = (((0,), (0,)), ((), ()))


def _params(*sem):
    return pltpu.CompilerParams(dimension_semantics=sem, vmem_limit_bytes=VMEM_LIMIT_BYTES)


def _dot(a, b):
    return jnp.dot(a, b, preferred_element_type=F32)


def _dot_nt(a, b):
    return lax.dot_general(a, b, NT_DIMS, preferred_element_type=F32)


def _dot_tn(a, b):
    return lax.dot_general(a, b, TN_DIMS, preferred_element_type=F32)


def _sigmoid(x):
    return 1.0 / (1.0 + jnp.exp(-x))


def _silu(x):
    return x * _sigmoid(x)


def _gelu_tanh(x):
    return 0.5 * x * (1.0 + jnp.tanh(0.7978845608028654 * (x + 0.044715 * (x * x * x))))


def _log_sigmoid(x):
    return jnp.minimum(x, 0.0) - jnp.log(1.0 + jnp.exp(-jnp.abs(x)))


def _rms(x, gain):
    ms = jnp.mean(x * x, axis=-1, keepdims=True)
    return x * lax.rsqrt(ms + EPS) * gain


def _layer_norm(x, gain, bias):
    mu = jnp.mean(x, axis=-1, keepdims=True)
    xc = x - mu
    var = jnp.mean(xc * xc, axis=-1, keepdims=True)
    return xc * lax.rsqrt(var + EPS) * gain + bias


def _split_bf16(x):
    hi = x.astype(BF16)
    lo = (x - hi.astype(F32)).astype(BF16)
    return hi, lo


def _row(v):
    return v.reshape(1, -1)


def _linear_kernel(*refs, has_gain, has_bias, has_res, act):
    refs = list(refs)
    x_ref = refs.pop(0)
    g_ref = refs.pop(0) if has_gain else None
    w_ref = refs.pop(0)
    b_ref = refs.pop(0) if has_bias else None
    r_ref = refs.pop(0) if has_res else None
    o_ref = refs.pop(0)

    if has_gain:
        xn = _rms(x_ref[...].astype(F32), g_ref[...]).astype(BF16)
    else:
        xn = x_ref[...].astype(BF16)
    y = _dot(xn, w_ref[...])
    if has_bias:
        y = y + b_ref[...]
    if act == "gelu":
        y = _gelu_tanh(y)
    elif act == "sigmoid":
        y = _sigmoid(y)
    if has_res:
        y = y + r_ref[...]
    o_ref[...] = y.astype(o_ref.dtype)


def _resident(shape):
    return pl.BlockSpec(shape, lambda *_: (0,) * len(shape), pipeline_mode=pl.Buffered(1))


def _linear(x, w, *, gain=None, bias=None, residual=None, act=None, out_dtype=F32, tm=512):
    m, k = x.shape
    n = w.shape[1]
    tm = min(tm, m)
    in_specs = [pl.BlockSpec((tm, k), lambda i: (i, 0))]
    args = [x]
    if gain is not None:
        in_specs.append(_resident((1, k)))
        args.append(_row(gain))
    in_specs.append(_resident((k, n)))
    args.append(w.astype(BF16))
    if bias is not None:
        in_specs.append(_resident((1, n)))
        args.append(_row(bias))
    if residual is not None:
        in_specs.append(pl.BlockSpec((tm, n), lambda i: (i, 0)))
        args.append(residual)
    kern = functools.partial(_linear_kernel, has_gain=gain is not None, has_bias=bias is not None,
                             has_res=residual is not None, act=act)
    return pl.pallas_call(
        kern, grid=(m // tm,), in_specs=in_specs,
        out_specs=pl.BlockSpec((tm, n), lambda i: (i, 0)),
        out_shape=jax.ShapeDtypeStruct((m, n), out_dtype),
        compiler_params=_params("parallel"),
        name="linear",
    )(*args)


def _linear_t_kernel(xt_ref, w_ref, r_ref, o_ref):
    o_ref[...] = r_ref[...] + _dot_tn(xt_ref[...], w_ref[...])


def _linear_t(x_t, w, residual, tm=512):
    batch, k, seq = x_t.shape
    n = w.shape[1]
    tm = min(tm, seq)
    nt = seq // tm
    return pl.pallas_call(
        _linear_t_kernel, grid=(batch, nt),
        in_specs=[
            pl.BlockSpec((pl.Squeezed(), k, tm), lambda b, i: (b, 0, i)),
            _resident((k, n)),
            pl.BlockSpec((tm, n), lambda b, i: (b * nt + i, 0)),
        ],
        out_specs=pl.BlockSpec((tm, n), lambda b, i: (b * nt + i, 0)),
        out_shape=jax.ShapeDtypeStruct((batch * seq, n), F32),
        compiler_params=_params("parallel", "parallel"),
        name="linear_t",
    )(x_t, w.astype(BF16), residual)


def _ffn_kernel(*refs, has_final):
    refs = list(refs)
    h_ref, g_ref, wg_ref, wu_ref, wd_ref = refs[:5]
    fg_ref = refs[5] if has_final else None
    o_ref = refs[-1]
    x = h_ref[...]
    xn = _rms(x, g_ref[...]).astype(BF16)
    a = (_silu(_dot(xn, wg_ref[...])) * _dot(xn, wu_ref[...])).astype(BF16)
    y = x + _dot(a, wd_ref[...])
    if has_final:
        y = _rms(y, fg_ref[...])
    o_ref[...] = y


def _ffn(h, gain, w_gate, w_up, w_down, final_gain=None, tm=512):
    m, d = h.shape
    ff = w_gate.shape[1]
    tm = min(tm, m)
    in_specs = [
        pl.BlockSpec((tm, d), lambda i: (i, 0)),
        _resident((1, d)), _resident((d, ff)), _resident((d, ff)), _resident((ff, d)),
    ]
    args = [h, _row(gain), w_gate.astype(BF16), w_up.astype(BF16), w_down.astype(BF16)]
    if final_gain is not None:
        in_specs.append(_resident((1, d)))
        args.append(_row(final_gain))
    return pl.pallas_call(
        functools.partial(_ffn_kernel, has_final=final_gain is not None),
        grid=(m // tm,), in_specs=in_specs,
        out_specs=pl.BlockSpec((tm, d), lambda i: (i, 0)),
        out_shape=jax.ShapeDtypeStruct((m, d), F32),
        compiler_params=_params("parallel"),
        name="ffn",
    )(*args)


def _gla_decay_kernel(h_ref, g_ref, w1_ref, w2_ref, b_ref, o_ref):
    xn = _rms(h_ref[...], g_ref[...]).astype(BF16)
    lr = _dot(xn, w1_ref[...]).astype(BF16)
    x = _dot(lr, w2_ref[...]) + b_ref[...]
    o_ref[...] = _log_sigmoid(x) * (1.0 / GLA_TAU)


def _gla_decay(h, gain, w_lr, w_gate2, b_gate, tm=512):
    m, d = h.shape
    rank = w_lr.shape[1]
    n = w_gate2.shape[1]
    tm = min(tm, m)
    w1 = jnp.zeros((d, LANES), BF16).at[:, :rank].set(w_lr.astype(BF16))
    w2 = jnp.zeros((LANES, n), BF16).at[:rank, :].set(w_gate2.astype(BF16))
    return pl.pallas_call(
        _gla_decay_kernel, grid=(m // tm,),
        in_specs=[
            pl.BlockSpec((tm, d), lambda i: (i, 0)),
            pl.BlockSpec((1, d), lambda i: (0, 0)),
            pl.BlockSpec((d, LANES), lambda i: (0, 0)),
            pl.BlockSpec((LANES, n), lambda i: (0, 0)),
            pl.BlockSpec((1, n), lambda i: (0, 0)),
        ],
        out_specs=pl.BlockSpec((tm, n), lambda i: (i, 0)),
        out_shape=jax.ShapeDtypeStruct((m, n), F32),
        compiler_params=_params("parallel"),
        name="gla_decay",
    )(h, _row(gain), w1, w2, _row(b_gate))


def _gla_core_kernel(q_ref, k_ref, v_ref, go_ref, la_ref, gn_ref, o_ref, st_ref, *, n_chunks, scale):
    c_len, sub = GLA_CHUNK, GLA_SUB
    n_sub = c_len // sub
    tr = n_chunks * c_len

    @pl.when(pl.program_id(2) == 0)
    def _():
        st_ref[...] = jnp.zeros_like(st_ref)

    r_io = lax.broadcasted_iota(jnp.int32, (tr, tr), 0)
    c_io = lax.broadcasted_iota(jnp.int32, (tr, tr), 1)
    tril = jnp.where((r_io >= c_io) & (r_io // c_len == c_io // c_len), 1.0, 0.0).astype(BF16)
    hi, lo = _split_bf16(la_ref[...])
    bcum = _dot(tril, hi) + _dot(tril, lo)
    q = q_ref[...] * scale
    k = k_ref[...]
    v = v_ref[...].astype(BF16)
    dk = q.shape[1]
    qe = (q * jnp.exp(bcum)).astype(BF16)

    causal = [lax.broadcasted_iota(jnp.int32, (sub, (i + 1) * sub), 1)
              <= lax.broadcasted_iota(jnp.int32, (sub, (i + 1) * sub), 0) + i * sub for i in range(n_sub)]
    qs, ks, kdecs, blasts = [], [], [], []
    for c in range(n_chunks):
        bc = bcum[c * c_len:(c + 1) * c_len]
        qc = q[c * c_len:(c + 1) * c_len]
        kc = k[c * c_len:(c + 1) * c_len]
        for i in range(n_sub):
            r0, r1 = i * sub, (i + 1) * sub
            base = bc[r0 - 1:r0, :] if i else jnp.zeros((1, dk), F32)
            qs.append((qc[r0:r1] * jnp.exp(bc[r0:r1] - base)).astype(BF16))
            ks.append((kc[:r1] * jnp.exp(base - bc[:r1])).astype(BF16))
        blast = bc[c_len - 1:c_len, :]
        blasts.append(blast)
        kdecs.append((kc * jnp.exp(blast - bc)).astype(BF16))
    scores = [_dot_nt(qi, ki) for qi, ki in zip(qs, ks)]
    scores = [jnp.where(causal[n % n_sub], a, 0.0).astype(BF16) for n, a in enumerate(scores)]
    intra = [_dot(a, v[(n // n_sub) * c_len:(n // n_sub) * c_len + (n % n_sub + 1) * sub])
             for n, a in enumerate(scores)]
    updates = [_dot_tn(v[c * c_len:(c + 1) * c_len], kdecs[c]) for c in range(n_chunks)]
    st = st_ref[...]
    inter = []
    for c in range(n_chunks):
        inter.append(_dot_nt(qe[c * c_len:(c + 1) * c_len], st.astype(BF16)))
        st = st * jnp.exp(blasts[c]) + updates[c]
    st_ref[...] = st
    o = jnp.concatenate(inter, axis=0) + jnp.concatenate(intra, axis=0)
    o_ref[...] = (_rms(o, gn_ref[...]) * _silu(go_ref[...])).astype(o_ref.dtype)


def _gla_core(proj, log_a, g_norm, batch, seq, tr=512):
    t = proj.shape[0]
    h_n = GLA_HEADS
    dk = log_a.shape[1] // h_n
    dv = (proj.shape[1] - 2 * h_n * dk) // (2 * h_n)
    tr = min(tr, seq)
    nt = seq // tr
    kb, vb, gb = h_n, (2 * h_n * dk) // dv, (2 * h_n * dk) // dv + h_n
    kern = functools.partial(_gla_core_kernel, n_chunks=tr // GLA_CHUNK, scale=dk ** -0.5)
    return pl.pallas_call(
        kern, grid=(batch, h_n, nt),
        in_specs=[
            pl.BlockSpec((tr, dk), lambda b, h, i: (b * nt + i, h)),
            pl.BlockSpec((tr, dk), lambda b, h, i: (b * nt + i, kb + h)),
            pl.BlockSpec((tr, dv), lambda b, h, i: (b * nt + i, vb + h)),
            pl.BlockSpec((tr, dv), lambda b, h, i: (b * nt + i, gb + h)),
            pl.BlockSpec((tr, dk), lambda b, h, i: (b * nt + i, h)),
            pl.BlockSpec((1, dv), lambda b, h, i: (0, 0)),
        ],
        out_specs=pl.BlockSpec((tr, dv), lambda b, h, i: (b * nt + i, h)),
        out_shape=jax.ShapeDtypeStruct((t, h_n * dv), BF16),
        scratch_shapes=[pltpu.VMEM((dv, dk), F32)],
        compiler_params=_params("parallel", "parallel", "arbitrary"),
        name="gla_core",
    )(proj, proj, proj, proj, log_a, _row(g_norm))


def _gla_mixer(h, gain, w_in, w_gate2, b_gate, g_norm, w_out, batch, seq):
    n_main = w_in.shape[1] - w_gate2.shape[0]
    proj = _linear(h, w_in[:, :n_main], gain=gain)
    log_a = _gla_decay(h, gain, w_in[:, n_main:], w_gate2, b_gate)
    o = _gla_core(proj, log_a, g_norm, batch, seq)
    return _linear(o, w_out, residual=h)


def _gmlp_sgu_kernel(u_ref, v_ref, lg_ref, lb_ref, ws_ref, bs_ref, wo_ref, h_ref, o_ref, p_ref, *, n_chunks):
    c_len = GMLP_CHUNK
    groups = ws_ref.shape[0]
    e = v_ref.shape[1]
    ge = e // groups
    causal = (lax.broadcasted_iota(jnp.int32, (c_len, c_len), 0)
              >= lax.broadcasted_iota(jnp.int32, (c_len, c_len), 1))
    for c in range(n_chunks):
        rows = slice(c * c_len, (c + 1) * c_len)
        vn = _layer_norm(v_ref[rows, :].astype(F32), lg_ref[...], lb_ref[...]).astype(BF16)
        for g in range(groups):
            cols = slice(g * ge, (g + 1) * ge)
            ws = jnp.where(causal, ws_ref[g], 0.0).astype(BF16)
            sv = _dot(ws, vn[:, cols]) + bs_ref[:, cols]
            p_ref[rows, cols] = (u_ref[rows, cols].astype(F32) * sv).astype(BF16)
    o_ref[...] = h_ref[...] + _dot(p_ref[...], wo_ref[...])


def _gmlp_mixer(h, gain, w_in, b_in, ln_g, ln_b, w_s, b_s, w_out, tm=256):
    m, d = h.shape
    e = w_out.shape[0]
    groups, c_len = w_s.shape[0], w_s.shape[1]
    z = _linear(h, w_in, gain=gain, bias=b_in, act="gelu", out_dtype=BF16)
    tm = min(tm, m)
    bs_full = jnp.repeat(b_s.T, e // groups, axis=1)
    return pl.pallas_call(
        functools.partial(_gmlp_sgu_kernel, n_chunks=tm // c_len),
        grid=(m // tm,),
        in_specs=[
            pl.BlockSpec((tm, e), lambda i: (i, 0)),
            pl.BlockSpec((tm, e), lambda i: (i, 1)),
            pl.BlockSpec((1, e), lambda i: (0, 0)),
            pl.BlockSpec((1, e), lambda i: (0, 0)),
            pl.BlockSpec((groups, c_len, c_len), lambda i: (0, 0, 0)),
            pl.BlockSpec((c_len, e), lambda i: (0, 0)),
            pl.BlockSpec((e, d), lambda i: (0, 0)),
            pl.BlockSpec((tm, d), lambda i: (i, 0)),
        ],
        out_specs=pl.BlockSpec((tm, d), lambda i: (i, 0)),
        out_shape=jax.ShapeDtypeStruct((m, d), F32),
        scratch_shapes=[pltpu.VMEM((tm, e), BF16)],
        compiler_params=_params("parallel"),
        name="gmlp_sgu",
    )(z, z, _row(ln_g), _row(ln_b), w_s, bs_full, w_out.astype(BF16), h)


def _conv_kernel(h_ref, halo_ref, g_ref, w1_ref, b1_ref, wdw_ref, bdw_ref, lg_ref, lb_ref, w2_ref, b2_ref,
                 o_ref, y_ref, c_ref, *, tm, rc, lc):
    d = h_ref.shape[1]
    halo = CONV_HALO

    def glu(x):
        ag = _dot(_rms(x, g_ref[...]).astype(BF16), w1_ref[...]) + b1_ref[...]
        return ag[:, :d] * _sigmoid(ag[:, d:])

    first = pl.program_id(1) == 0
    y_ref[0:halo, :] = jnp.where(first, 0.0, glu(halo_ref[...]))
    y_ref[halo:, :] = glu(h_ref[...])
    off = halo - (CONV_WIDTH - 1)
    sl = SUBLANES
    for r in range(tm // rc):
        for lb in range(d // lc):
            lanes = slice(lb * lc, (lb + 1) * lc)
            out = None
            for phase in range(sl):
                rows = rc + (sl if phase else 0)
                part = None
                for j in range(CONV_WIDTH):
                    if (off + j) % sl != phase:
                        continue
                    a = r * rc + (off + j) // sl * sl
                    win = y_ref[a:a + rows, lanes].reshape(rows // sl, sl, lc)
                    term = win * wdw_ref[sl * j:sl * (j + 1), lanes][None]
                    part = term if part is None else part + term
                part = part.reshape(rows, lc)[phase:phase + rc]
                out = part if out is None else out + part
            c_ref[r * rc:(r + 1) * rc, lanes] = out
    yc = _layer_norm(c_ref[...] + bdw_ref[...], lg_ref[...], lb_ref[...])
    o_ref[...] = h_ref[...] + _dot(_silu(yc).astype(BF16), w2_ref[...]) + b2_ref[...]


def _conv_mixer(h, gain, w_pw1, b_pw1, w_dw, b_dw, ln_g, ln_b, w_pw2, b_pw2, batch, seq, tm=512, rc=32, lc=256):
    m, d = h.shape
    tm = min(tm, seq)
    nt = seq // tm
    hb = tm // CONV_HALO
    kern = functools.partial(_conv_kernel, tm=tm, rc=rc, lc=lc)
    return pl.pallas_call(
        kern, grid=(batch, nt),
        in_specs=[
            pl.BlockSpec((tm, d), lambda b, i: (b * nt + i, 0)),
            pl.BlockSpec((CONV_HALO, d), lambda b, i: (jnp.maximum((b * nt + i) * hb - 1, 0), 0)),
            pl.BlockSpec((1, d), lambda b, i: (0, 0)),
            pl.BlockSpec((d, 2 * d), lambda b, i: (0, 0)),
            pl.BlockSpec((1, 2 * d), lambda b, i: (0, 0)),
            pl.BlockSpec((CONV_WIDTH * SUBLANES, d), lambda b, i: (0, 0)),
            pl.BlockSpec((1, d), lambda b, i: (0, 0)),
            pl.BlockSpec((1, d), lambda b, i: (0, 0)),
            pl.BlockSpec((1, d), lambda b, i: (0, 0)),
            pl.BlockSpec((d, d), lambda b, i: (0, 0)),
            pl.BlockSpec((1, d), lambda b, i: (0, 0)),
        ],
        out_specs=pl.BlockSpec((tm, d), lambda b, i: (b * nt + i, 0)),
        out_shape=jax.ShapeDtypeStruct((m, d), F32),
        scratch_shapes=[pltpu.VMEM((tm + CONV_HALO, d), F32), pltpu.VMEM((tm, d), F32)],
        compiler_params=_params("parallel", "arbitrary"),
        name="conv_mixer",
    )(h, h, _row(gain), w_pw1.astype(BF16), _row(b_pw1), jnp.repeat(w_dw, SUBLANES, axis=0),
      _row(b_dw), _row(ln_g), _row(ln_b),
      w_pw2.astype(BF16), _row(b_pw2))


def _nsa_compress_kernel(x_ref, plo_ref, phi_ref, w1a_ref, w1b_ref, w2_ref, w2t_ref, k_ref, vt_ref):
    x = x_ref[...]
    a = _dot((x + plo_ref[...]).astype(BF16), w1a_ref[...])
    b = _dot((x + phi_ref[...]).astype(BF16), w1b_ref[...])
    b_next = jnp.concatenate([b[1:], b[:1]], axis=0)
    hid = _gelu_tanh(a + b_next).astype(BF16)
    k_ref[...] = _dot(hid, w2_ref[...]).astype(k_ref.dtype)
    vt_ref[...] = _dot_nt(w2t_ref[...], hid).astype(vt_ref.dtype)


def _nsa_compress(x2, pe, w1, w2):
    two, groups, batch, ns, kd = x2.shape
    dh = w2.shape[2]
    hid = w1.shape[2]
    plo = pe[:, :NSA_CMP_STRIDE].reshape(two, 1, kd)
    phi = pe[:, NSA_CMP_STRIDE:].reshape(two, 1, kd)
    w1a = w1[:, :kd].astype(BF16)
    w1b = w1[:, kd:].astype(BF16)
    w2b = w2.astype(BF16)
    w2t = jnp.swapaxes(w2, 1, 2).astype(BF16)
    sq = pl.Squeezed()
    return pl.pallas_call(
        _nsa_compress_kernel, grid=(two, batch, groups),
        in_specs=[
            pl.BlockSpec((sq, sq, sq, ns, kd), lambda c, b, g: (c, g, b, 0, 0)),
            pl.BlockSpec((sq, 1, kd), lambda c, b, g: (c, 0, 0)),
            pl.BlockSpec((sq, 1, kd), lambda c, b, g: (c, 0, 0)),
            pl.BlockSpec((sq, kd, hid), lambda c, b, g: (c, 0, 0)),
            pl.BlockSpec((sq, kd, hid), lambda c, b, g: (c, 0, 0)),
            pl.BlockSpec((sq, hid, dh), lambda c, b, g: (c, 0, 0)),
            pl.BlockSpec((sq, dh, hid), lambda c, b, g: (c, 0, 0)),
        ],
        out_specs=[
            pl.BlockSpec((sq, sq, sq, ns, dh), lambda c, b, g: (c, b, g, 0, 0)),
            pl.BlockSpec((sq, sq, sq, dh, ns), lambda c, b, g: (c, b, g, 0, 0)),
        ],
        out_shape=[
            jax.ShapeDtypeStruct((two, batch, groups, ns, dh), BF16),
            jax.ShapeDtypeStruct((two, batch, groups, dh, ns), BF16),
        ],
        compiler_params=_params("parallel", "parallel", "parallel"),
        name="nsa_compress",
    )(x2, plo, phi, w1a, w1b, w2b, w2t)


def _nsa_select_kernel(sl_ref, q_ref, kc_ref, vct_ref, ovt_ref, pool_ref, oc_ref, selb_ref, flag_ref,
                       imp_ref, x_ref, cnt_ref, *, n_sel, top_n):
    qb, dh, sel_blk = NSA_Q_BLOCK, NSA_HEAD_DIM, NSA_SEL_BLOCK
    rep = q_ref.shape[1] // dh
    g_idx = pl.program_id(1)
    i_blk = pl.program_id(2)
    q0 = i_blk * qb
    nc_pad = kc_ref.shape[0]
    ch = min(NSA_CMP_CHUNK, nc_pad)

    q_tile = q_ref[...]
    q = jnp.concatenate([q_tile[:, r * dh:(r + 1) * dh] for r in range(rep)], axis=0)
    slopes = [sl_ref[g_idx * rep + r] for r in range(rep)]
    t_row = q0 + lax.broadcasted_iota(jnp.int32, (1, qb), 1)
    n_last = (q0 + qb - NSA_CMP_BLOCK) // NSA_CMP_STRIDE

    def compressed_branch(n_rows):
        n_col = lax.broadcasted_iota(jnp.int32, (n_rows, 1), 0)
        c_start = n_col * NSA_CMP_STRIDE
        mask = (c_start + (NSA_CMP_BLOCK - 1) <= t_row) & (n_col < nc_pad - 1)
        dist = t_row.astype(F32) - (c_start.astype(F32) + (NSA_CMP_BLOCK - 1) / 2.0)
        s_all = _dot_nt(kc_ref[0:n_rows, :], q)
        p_sum = jnp.zeros((n_rows, qb), F32)
        p_parts = []
        for r in range(rep):
            s = jnp.where(mask, s_all[:, r * qb:(r + 1) * qb] - slopes[r] * dist, NEG_BIG)
            m = jnp.max(s, axis=0, keepdims=True)
            e = jnp.where(mask, jnp.exp(s - m), 0.0)
            l = jnp.sum(e, axis=0, keepdims=True)
            p = e * jnp.where(l > 0.0, 1.0 / l, 0.0)
            p_sum = p_sum + p
            p_parts.append(p.astype(BF16))
        o_c = _dot(vct_ref[:, 0:n_rows], jnp.concatenate(p_parts, axis=1))
        for r in range(rep):
            oc_ref[r * dh:(r + 1) * dh, :] = o_c[:, r * qb:(r + 1) * qb]
        p_hi, p_lo = _split_bf16(p_sum)
        imp_ref[...] = _dot(ovt_ref[:, 0:n_rows], p_hi) + _dot(ovt_ref[:, 0:n_rows], p_lo)

    n_variants = nc_pad // ch
    need = jnp.minimum(n_last // ch, n_variants - 1)
    for v in range(n_variants):
        @pl.when(need == v)
        def _():
            compressed_branch((v + 1) * ch)

    imp = imp_ref[...]

    j_col = lax.broadcasted_iota(jnp.int32, (n_sel, 1), 0)
    cur = t_row // sel_blk
    forced = (j_col == 0) | (j_col == cur) | (j_col == cur - 1)
    valid = j_col * sel_blk <= t_row
    x_ref[...] = jnp.where(forced, jnp.inf, jnp.where(valid, imp, -jnp.inf))
    cnt_ref[...] = jnp.zeros(cnt_ref.shape, jnp.int32)
    n_live = (q0 + qb - 1) // sel_blk + 1
    ic = min(NSA_RANK_CHUNK, n_sel)
    rq = min(NSA_RANK_ROWS, n_sel)
    jj = lax.broadcasted_iota(jnp.int32, (SUBLANES, 1), 0)
    for c0 in range(0, n_sel, ic):
        @pl.when(c0 < n_live)
        def _():
            for r0 in range(0, n_sel, rq):
                @pl.when(r0 < n_live)
                def _():
                    groups = range(r0 // SUBLANES, (r0 + rq) // SUBLANES)
                    xg = {a: x_ref[SUBLANES * a:SUBLANES * (a + 1), :] for a in groups}
                    cnt = {a: cnt_ref[SUBLANES * a:SUBLANES * (a + 1), :] for a in groups}
                    for i in range(c0, c0 + ic):
                        xi = x_ref[i:i + 1, :]
                        for a in groups:
                            if SUBLANES * a > i:
                                cnt[a] = cnt[a] + jnp.where(xi >= xg[a], 1, 0)
                            elif SUBLANES * a + SUBLANES - 1 < i:
                                cnt[a] = cnt[a] + jnp.where(xi > xg[a], 1, 0)
                            else:
                                tie = jnp.where(jj + SUBLANES * a > i, 1, 0)
                                cnt[a] = cnt[a] + jnp.where(xi > xg[a], 1, jnp.where(xi == xg[a], tie, 0))
                    for a in groups:
                        cnt_ref[SUBLANES * a:SUBLANES * (a + 1), :] = cnt[a]
    chosen = valid & (cnt_ref[...] < top_n)
    selb_ref[...] = jnp.where(chosen, 0.0, NEG_BIG)
    per_block = _dot_nt(jnp.ones((SUBLANES, qb), BF16), jnp.where(chosen, 1.0, 0.0).astype(BF16))
    flag_ref[...] = _dot(per_block.astype(BF16), pool_ref[...])


def _nsa_attend_kernel(sl_ref, flag_ref, q_ref, qa_ref, ks_ref, vst_ref, kw_ref, vwt_ref, selb_ref, oc_ref,
                       gt_ref, o_ref, m_ref, l_ref, acc_ref, list_ref):
    qb, dh, kt, sub = NSA_Q_BLOCK, NSA_HEAD_DIM, NSA_KEY_TILE, NSA_SEL_BLOCK
    nb = kt // sub
    rep = q_ref.shape[1] // dh
    g_idx = pl.program_id(1)
    q0 = pl.program_id(2) * qb
    kd = q0 // kt

    q_tile = q_ref[...]
    qp = jnp.concatenate(
        [jnp.concatenate([q_tile[:, r * dh:(r + 1) * dh], jnp.broadcast_to(qa_ref[r:r + 1, :], (qb, dh))], axis=1)
         for r in range(rep)], axis=0)
    slopes = [sl_ref[g_idx * rep + r] for r in range(rep)]
    d_rel = lax.broadcasted_iota(jnp.int32, (kt, qb), 1) - lax.broadcasted_iota(jnp.int32, (kt, qb), 0)

    def update(k_ref, vt_ref, specs, state):
        return absorb(vt_ref, specs, scores(k_ref, specs), state)

    def scores(k_ref, specs):
        return [_dot_nt(k_ref[pl.ds(pl.multiple_of(kb * kt, kt), kt), :], qp) for kb, _ in specs]

    def absorb(vt_ref, specs, raws, state):
        k0s = [pl.multiple_of(kb * kt, kt) for kb, _ in specs]
        offs = [(q0 - k0).astype(F32) for k0 in k0s]
        m_out, l_out, acc_out = [], [], []
        for r in range(rep):
            cols = slice(r * qb, (r + 1) * qb)
            cs = [-slopes[r] * off for off in offs]
            ss = [jnp.concatenate([raw[b * sub:(b + 1) * sub, cols] + bias_fn(b) for b in range(nb)], axis=0)
                  for raw, (_, bias_fn) in zip(raws, specs)]
            m_new = None if state is None else state[0][:, cols]
            for s, c in zip(ss, cs):
                top = jnp.max(s, axis=0, keepdims=True) + c
                m_new = top if m_new is None else jnp.maximum(m_new, top)
            l_new = acc = None
            if state is not None:
                alpha = jnp.exp(state[0][:, cols] - m_new)
                l_new = alpha * state[1][:, cols]
                acc = alpha * state[2][:, cols]
            for s, c, k0 in zip(ss, cs, k0s):
                e = jnp.exp(s - (m_new - c))
                e_sum = jnp.sum(e, axis=0, keepdims=True)
                l_new = e_sum if l_new is None else l_new + e_sum
                pv = _dot(vt_ref[:, pl.ds(k0, kt)], e.astype(BF16))
                acc = pv if acc is None else acc + pv
            m_out.append(m_new)
            l_out.append(l_new)
            acc_out.append(acc)
        return jnp.concatenate(m_out, axis=1), jnp.concatenate(l_out, axis=1), jnp.concatenate(acc_out, axis=1)

    def causal_bias(kb):
        return jnp.where(d_rel + (q0 - kb * kt) >= 0, 0.0, NEG_BIG)

    def sel_spec(kb, causal):
        if causal is None:
            return kb, lambda b: selb_ref[pl.ds(kb * nb + b, 1), :]
        return kb, lambda b: selb_ref[pl.ds(kb * nb + b, 1), :] + causal[b * sub:(b + 1) * sub]

    def win_spec(kb):
        dist = d_rel + (q0 - kb * kt)
        wb = jnp.where((dist >= 0) & (dist < NSA_WINDOW), 0.0, NEG_BIG)
        return kb, lambda b: wb[b * sub:(b + 1) * sub]

    near = jnp.where(kd > 0, kd - 1, 1)
    n_win = -(-NSA_WINDOW // kt) + 1
    first = jnp.clip(kd - (n_win - 1), 0, ks_ref.shape[0] // kt - n_win)
    near_specs = [sel_spec(kd, causal_bias(kd)), sel_spec(near, causal_bias(near))]
    win_specs = [win_spec(first + j) for j in range(n_win)]
    near_raw = scores(ks_ref, near_specs)
    win_raw = scores(kw_ref, win_specs)
    sel_state = absorb(vst_ref, near_specs, near_raw, None)
    _, l_w, acc_w = absorb(vwt_ref, win_specs, win_raw, None)
    o_w = acc_w * (1.0 / l_w)
    m_ref[...], l_ref[...], acc_ref[...] = sel_state

    def compact(kb, n):
        list_ref[n] = kb
        return n + (flag_ref[0, kb] > 0).astype(jnp.int32)

    n_act = lax.fori_loop(0, kd - 1, compact, jnp.int32(0))

    def step(kbs):
        state = update(ks_ref, vst_ref, [sel_spec(kb, None) for kb in kbs],
                       (m_ref[...], l_ref[...], acc_ref[...]))
        m_ref[...], l_ref[...], acc_ref[...] = state

    def pair(i, carry):
        step([list_ref[2 * i], list_ref[2 * i + 1]])
        return carry

    lax.fori_loop(0, n_act // 2, pair, 0)

    @pl.when(n_act % 2 == 1)
    def _():
        step([list_ref[n_act - 1]])

    o_s = acc_ref[...] * (1.0 / l_ref[...])

    for r in range(rep):
        cols = slice(r * qb, (r + 1) * qb)
        rows = slice(r * dh, (r + 1) * dh)
        g_c = gt_ref[3 * r:3 * r + 1, :]
        g_s = gt_ref[3 * r + 1:3 * r + 2, :]
        g_w = gt_ref[3 * r + 2:3 * r + 3, :]
        o_ref[rows, :] = (g_c * oc_ref[rows, :] + g_s * o_s[:, cols] + g_w * o_w[:, cols]).astype(o_ref.dtype)


def _alibi_slopes(n):
    return 2.0 ** (-8.0 * jnp.arange(1, n + 1, dtype=F32) / n)


def _nsa_overlap_t(nc_pad, n_sel):
    c_start = NSA_CMP_STRIDE * jnp.arange(nc_pad, dtype=jnp.int32)
    s_start = NSA_SEL_BLOCK * jnp.arange(n_sel, dtype=jnp.int32)
    ov = jnp.clip(jnp.minimum(c_start[:, None] + NSA_CMP_BLOCK, s_start[None] + NSA_SEL_BLOCK)
                  - jnp.maximum(c_start[:, None], s_start[None]), 0, None).astype(F32) / NSA_CMP_BLOCK
    return ov.T.astype(BF16)


def _nsa_proj_kernel(h_ref, g_ref, wqk_ref, wvt_ref, wc_ref, wgt_ref,
                     q_ref, ks_ref, kw_ref, vst_ref, vwt_ref, c_ref, gt_ref):
    groups, tm, dh2 = ks_ref.shape
    dh = dh2 // 2
    n_q = q_ref.shape[1]
    n_kv = groups * dh
    xn = _rms(h_ref[...], g_ref[...]).astype(BF16)
    y = _dot(xn, wqk_ref[...])
    q_ref[...] = y[:, :n_q].astype(q_ref.dtype)
    row = pl.program_id(0) * tm + lax.broadcasted_iota(jnp.int32, (tm, dh), 0)
    lane = lax.broadcasted_iota(jnp.int32, (tm, dh), 1)
    feat = jnp.where(lane < 3, (row % NSA_KEY_TILE).astype(F32), 0.0).astype(ks_ref.dtype)
    for g in range(groups):
        ks_ref[g, :, 0:dh] = y[:, n_q + g * dh:n_q + (g + 1) * dh].astype(ks_ref.dtype)
        ks_ref[g, :, dh:dh2] = feat
        kw_ref[g, :, 0:dh] = y[:, n_q + n_kv + g * dh:n_q + n_kv + (g + 1) * dh].astype(kw_ref.dtype)
        kw_ref[g, :, dh:dh2] = feat
    vt = _dot_nt(wvt_ref[...], xn)
    vst_ref[...] = vt[:n_kv].astype(vst_ref.dtype)
    vwt_ref[...] = vt[n_kv:].astype(vwt_ref.dtype)
    c = _dot(xn, wc_ref[...])
    for cg in range(2 * groups):
        c_ref[cg] = c[:, cg * dh:(cg + 1) * dh]
    gt_ref[...] = _sigmoid(_dot_nt(wgt_ref[...], xn))


def _nsa_proj(h, gain, w_in, batch, seq, tm=512):
    m, d = h.shape
    heads, groups, dh = NSA_HEADS, NSA_KV_HEADS, NSA_HEAD_DIM
    n_q, n_kv = heads * dh, groups * dh
    n_gate = w_in.shape[1] - (n_q + 6 * n_kv)
    tm = min(tm, seq)
    nt = seq // tm
    kv0 = n_q
    w_qk = jnp.concatenate([w_in[:, :n_q] * dh ** -0.5, w_in[:, kv0 + 2 * n_kv:kv0 + 3 * n_kv],
                            w_in[:, kv0 + 4 * n_kv:kv0 + 5 * n_kv]], axis=1).astype(BF16)
    w_vt = jnp.concatenate([w_in[:, kv0 + 3 * n_kv:kv0 + 4 * n_kv],
                            w_in[:, kv0 + 5 * n_kv:kv0 + 6 * n_kv]], axis=1).T.astype(BF16)
    w_c = w_in[:, kv0:kv0 + 2 * n_kv].astype(BF16)
    w_gt = w_in[:, kv0 + 6 * n_kv:].T.astype(BF16)
    sq = pl.Squeezed()
    return pl.pallas_call(
        _nsa_proj_kernel, grid=(m // tm,),
        in_specs=[
            pl.BlockSpec((tm, d), lambda i: (i, 0)),
            _resident((1, d)), _resident(w_qk.shape), _resident(w_vt.shape), _resident(w_c.shape),
            _resident(w_gt.shape),
        ],
        out_specs=[
            pl.BlockSpec((tm, n_q), lambda i: (i, 0)),
            pl.BlockSpec((groups, tm, 2 * dh), lambda i: (0, i, 0)),
            pl.BlockSpec((groups, tm, 2 * dh), lambda i: (0, i, 0)),
            pl.BlockSpec((sq, n_kv, tm), lambda i: (i // nt, 0, i % nt)),
            pl.BlockSpec((sq, n_kv, tm), lambda i: (i // nt, 0, i % nt)),
            pl.BlockSpec((2 * groups, tm, dh), lambda i: (0, i, 0)),
            pl.BlockSpec((sq, n_gate, tm), lambda i: (i // nt, 0, i % nt)),
        ],
        out_shape=[
            jax.ShapeDtypeStruct((m, n_q), BF16),
            jax.ShapeDtypeStruct((groups, m, 2 * dh), BF16),
            jax.ShapeDtypeStruct((groups, m, 2 * dh), BF16),
            jax.ShapeDtypeStruct((batch, n_kv, seq), BF16),
            jax.ShapeDtypeStruct((batch, n_kv, seq), BF16),
            jax.ShapeDtypeStruct((2 * groups, m, dh), F32),
            jax.ShapeDtypeStruct((batch, n_gate, seq), F32),
        ],
        compiler_params=_params("parallel"),
        name="nsa_proj",
    )(h, _row(gain), w_qk, w_vt, w_c, w_gt)


def _nsa_mixer(h, gain, w_in, pe_k, w_ck1, w_ck2, pe_v, w_cv1, w_cv2, w_out, batch, seq):
    m, d = h.shape
    heads, groups, dh = NSA_HEADS, NSA_KV_HEADS, NSA_HEAD_DIM
    rep = heads // groups
    qb = NSA_Q_BLOCK
    n_q = heads * dh
    n_kv = groups * dh
    n_sel = seq // NSA_SEL_BLOCK
    nc_pad = seq // NSA_CMP_STRIDE
    top_n = min(NSA_TOP_N, n_sel)

    att, k_sel_aug, k_win_aug, v_sel_t, v_win_t, cmp_src, gates_t = _nsa_proj(h, gain, w_in, batch, seq)
    v_sel_t = v_sel_t.reshape(batch, groups, dh, seq)
    v_win_t = v_win_t.reshape(batch, groups, dh, seq)
    gates_t = gates_t.reshape(batch, groups, rep * 3, seq)
    x2 = cmp_src.reshape(2, groups, batch, nc_pad, NSA_CMP_STRIDE * dh)
    k_cmp_both, v_cmp_t_both = _nsa_compress(
        x2, jnp.stack([pe_k, pe_v]), jnp.stack([w_ck1, w_cv1]), jnp.stack([w_ck2, w_cv2]))
    k_cmp, v_cmp_t = k_cmp_both[0], v_cmp_t_both[1]

    sq = pl.Squeezed()
    nq = seq // qb
    kt = NSA_KEY_TILE
    n_tiles = seq // kt
    slopes = _alibi_slopes(heads)
    pool = (jnp.arange(n_sel)[:, None] // (kt // NSA_SEL_BLOCK) == jnp.arange(LANES)[None, :]).astype(BF16)
    o_c, sel_bias, tile_hits = pl.pallas_call(
        functools.partial(_nsa_select_kernel, n_sel=n_sel, top_n=top_n),
        grid=(batch, groups, nq),
        in_specs=[
            pl.BlockSpec(memory_space=pltpu.SMEM),
            pl.BlockSpec((qb, rep * dh), lambda b, g, i: (b * nq + i, g)),
            pl.BlockSpec((sq, sq, nc_pad, dh), lambda b, g, i: (b, g, 0, 0)),
            pl.BlockSpec((sq, sq, dh, nc_pad), lambda b, g, i: (b, g, 0, 0)),
            pl.BlockSpec((n_sel, nc_pad), lambda b, g, i: (0, 0)),
            pl.BlockSpec((n_sel, LANES), lambda b, g, i: (0, 0)),
        ],
        out_specs=[
            pl.BlockSpec((sq, rep * dh, qb), lambda b, g, i: (b, g, i)),
            pl.BlockSpec((sq, sq, n_sel, qb), lambda b, g, i: (b, g, 0, i)),
            pl.BlockSpec((sq, sq, sq, 8, LANES), lambda b, g, i: (b, g, i, 0, 0)),
        ],
        out_shape=[
            jax.ShapeDtypeStruct((batch, heads * dh, seq), F32),
            jax.ShapeDtypeStruct((batch, groups, n_sel, seq), F32),
            jax.ShapeDtypeStruct((batch, groups, nq, 8, LANES), F32),
        ],
        scratch_shapes=[pltpu.VMEM((n_sel, qb), F32), pltpu.VMEM((n_sel, qb), F32),
                        pltpu.VMEM((n_sel, qb), jnp.int32)],
        compiler_params=_params("parallel", "parallel", "parallel"),
        name="nsa_select",
    )(slopes, att, k_cmp, v_cmp_t, _nsa_overlap_t(nc_pad, n_sel), pool)
    tile_flags = (tile_hits[:, :, :, 0, :n_tiles] > 0.5).astype(jnp.int32).reshape(batch * groups * nq, 1, n_tiles)

    s_hi = slopes.astype(BF16)
    s_mid = (slopes - s_hi.astype(F32)).astype(BF16)
    s_lo = (slopes - s_hi.astype(F32) - s_mid.astype(F32)).astype(BF16)
    q_aug = jnp.zeros((heads, dh), BF16).at[:, 0].set(s_hi).at[:, 1].set(s_mid).at[:, 2].set(s_lo)
    q_aug = q_aug.reshape(groups, rep, dh)

    o_t = pl.pallas_call(
        _nsa_attend_kernel, grid=(batch, groups, nq),
        in_specs=[
            pl.BlockSpec(memory_space=pltpu.SMEM),
            pl.BlockSpec((sq, 1, n_tiles), lambda b, g, i: ((b * groups + g) * nq + i, 0, 0),
                         memory_space=pltpu.SMEM),
            pl.BlockSpec((qb, rep * dh), lambda b, g, i: (b * nq + i, g)),
            pl.BlockSpec((sq, rep, dh), lambda b, g, i: (g, 0, 0)),
            pl.BlockSpec((sq, seq, 2 * dh), lambda b, g, i: (g, b, 0)),
            pl.BlockSpec((sq, sq, dh, seq), lambda b, g, i: (b, g, 0, 0)),
            pl.BlockSpec((sq, seq, 2 * dh), lambda b, g, i: (g, b, 0)),
            pl.BlockSpec((sq, sq, dh, seq), lambda b, g, i: (b, g, 0, 0)),
            pl.BlockSpec((sq, sq, n_sel, qb), lambda b, g, i: (b, g, 0, i)),
            pl.BlockSpec((sq, rep * dh, qb), lambda b, g, i: (b, g, i)),
            pl.BlockSpec((sq, sq, rep * 3, qb), lambda b, g, i: (b, g, 0, i)),
        ],
        out_specs=pl.BlockSpec((sq, rep * dh, qb), lambda b, g, i: (b, g, i)),
        out_shape=jax.ShapeDtypeStruct((batch, heads * dh, seq), BF16),
        scratch_shapes=[pltpu.VMEM((1, rep * qb), F32), pltpu.VMEM((1, rep * qb), F32),
                        pltpu.VMEM((dh, rep * qb), F32), pltpu.SMEM((n_tiles,), jnp.int32)],
        compiler_params=_params("parallel", "parallel", "arbitrary"),
        name="nsa_attend",
    )(slopes, tile_flags, att, q_aug, k_sel_aug, v_sel_t, k_win_aug, v_win_t, sel_bias, o_c, gates_t)
    return _linear_t(o_t, w_out, h)


def kernel(x, norm_mix, norm_ffn, norm_final, w_ffn_gate, w_ffn_up, w_ffn_down, a_w_in, a_w_gate2, a_b_gate, a_g_norm, a_w_out, b_w_in, b_b_in, b_ln_g, b_ln_b, b_w_s, b_b_s, b_w_out, c_w_pw1, c_b_pw1, c_w_dw, c_b_dw, c_ln_g, c_ln_b, c_w_pw2, c_b_pw2, d_w_in, d_pe_k, d_w_ck1, d_w_ck2, d_pe_v, d_w_cv1, d_w_cv2, d_w_out):
    batch, seq, d = x.shape
    depth = norm_mix.shape[0]
    h = x.reshape(batch * seq, d)
    for i in range(depth):
        m, j = i % 4, i // 4
        if m == 0:
            h = _gla_mixer(h, norm_mix[i], a_w_in[j], a_w_gate2[j], a_b_gate[j], a_g_norm[j], a_w_out[j],
                           batch, seq)
        elif m == 1:
            h = _gmlp_mixer(h, norm_mix[i], b_w_in[j], b_b_in[j], b_ln_g[j], b_ln_b[j], b_w_s[j], b_b_s[j],
                            b_w_out[j])
        elif m == 2:
            h = _conv_mixer(h, norm_mix[i], c_w_pw1[j], c_b_pw1[j], c_w_dw[j], c_b_dw[j], c_ln_g[j],
                            c_ln_b[j], c_w_pw2[j], c_b_pw2[j], batch, seq)
        else:
            h = _nsa_mixer(h, norm_mix[i], d_w_in[j], d_pe_k[j], d_w_ck1[j], d_w_ck2[j], d_pe_v[j],
                           d_w_cv1[j], d_w_cv2[j], d_w_out[j], batch, seq)
        final = norm_final if i == depth - 1 else None
        h = _ffn(h, norm_ffn[i], w_ffn_gate[i], w_ffn_up[i], w_ffn_down[i], final_gain=final)
    return h.reshape(batch, seq, d)
```

```python
import functools

import jax
import jax.numpy as jnp
from jax import lax
from jax.experimental import pallas as pl
from jax.experimental.pallas import tpu as pltpu

F32 = jnp.float32
BF16 = jnp.bfloat16

EPS = 1e-6
V7X_VMEM_BYTES = 64 * 1024 * 1024
VMEM_LIMIT_BYTES = V7X_VMEM_BYTES - 8 * 1024 * 1024
LANES = 128
SUBLANES = 8

GLA_HEADS = 4
GLA_TAU = 16.0
GLA_CHUNK = 64
GLA_SUB = 16
GMLP_GROUPS = 8
GMLP_CHUNK = 128
CONV_WIDTH = 31
CONV_HALO = 32
NSA_HEADS = 16
NSA_KV_HEADS = 4
NSA_HEAD_DIM = 64
NSA_CMP_BLOCK = 32
NSA_CMP_STRIDE = 16
NSA_SEL_BLOCK = 64
NSA_TOP_N = 16
NSA_WINDOW = 512
NSA_Q_BLOCK = 128
NSA_KEY_TILE = 256
NSA_CMP_CHUNK = 128
NSA_RANK_CHUNK = 32
NSA_RANK_ROWS = 64
NEG_BIG = -1e30

NT_DIMS = (((1,), (1,)), ((), ()))
TN_DIMS = (((0,), (0,)), ((), ()))


def _params(*sem):
    return pltpu.CompilerParams(dimension_semantics=sem, vmem_limit_bytes=VMEM_LIMIT_BYTES)


def _dot(a, b):
    return jnp.dot(a, b, preferred_element_type=F32)


def _dot_nt(a, b):
    return lax.dot_general(a, b, NT_DIMS, preferred_element_type=F32)


def _dot_tn(a, b):
    return lax.dot_general(a, b, TN_DIMS, preferred_element_type=F32)


def _sigmoid(x):
    return 1.0 / (1.0 + jnp.exp(-x))


def _silu(x):
    return x * _sigmoid(x)


def _gelu_tanh(x):
    return 0.5 * x * (1.0 + jnp.tanh(0.7978845608028654 * (x + 0.044715 * (x * x * x))))


def _log_sigmoid(x):
    return jnp.minimum(x, 0.0) - jnp.log(1.0 + jnp.exp(-jnp.abs(x)))


def _rms(x, gain):
    ms = jnp.mean(x * x, axis=-1, keepdims=True)
    return x * lax.rsqrt(ms + EPS) * gain


def _layer_norm(x, gain, bias):
    mu = jnp.mean(x, axis=-1, keepdims=True)
    xc = x - mu
    var = jnp.mean(xc * xc, axis=-1, keepdims=True)
    return xc * lax.rsqrt(var + EPS) * gain + bias


def _split_bf16(x):
    hi = x.astype(BF16)
    lo = (x - hi.astype(F32)).astype(BF16)
    return hi, lo


def _row(v):
    return v.reshape(1, -1)


def _linear_kernel(*refs, has_gain, has_bias, has_res, act):
    refs = list(refs)
    x_ref = refs.pop(0)
    g_ref = refs.pop(0) if has_gain else None
    w_ref = refs.pop(0)
    b_ref = refs.pop(0) if has_bias else None
    r_ref = refs.pop(0) if has_res else None
    o_ref = refs.pop(0)

    if has_gain:
        xn = _rms(x_ref[...].astype(F32), g_ref[...]).astype(BF16)
    else:
        xn = x_ref[...].astype(BF16)
    y = _dot(xn, w_ref[...])
    if has_bias:
        y = y + b_ref[...]
    if act == "gelu":
        y = _gelu_tanh(y)
    elif act == "sigmoid":
        y = _sigmoid(y)
    if has_res:
        y = y + r_ref[...]
    o_ref[...] = y.astype(o_ref.dtype)


def _resident(shape):
    return pl.BlockSpec(shape, lambda *_: (0,) * len(shape), pipeline_mode=pl.Buffered(1))


def _linear(x, w, *, gain=None, bias=None, residual=None, act=None, out_dtype=F32, tm=512):
    m, k = x.shape
    n = w.shape[1]
    tm = min(tm, m)
    in_specs = [pl.BlockSpec((tm, k), lambda i: (i, 0))]
    args = [x]
    if gain is not None:
        in_specs.append(_resident((1, k)))
        args.append(_row(gain))
    in_specs.append(_resident((k, n)))
    args.append(w.astype(BF16))
    if bias is not None:
        in_specs.append(_resident((1, n)))
        args.append(_row(bias))
    if residual is not None:
        in_specs.append(pl.BlockSpec((tm, n), lambda i: (i, 0)))
        args.append(residual)
    kern = functools.partial(_linear_kernel, has_gain=gain is not None, has_bias=bias is not None,
                             has_res=residual is not None, act=act)
    return pl.pallas_call(
        kern, grid=(m // tm,), in_specs=in_specs,
        out_specs=pl.BlockSpec((tm, n), lambda i: (i, 0)),
        out_shape=jax.ShapeDtypeStruct((m, n), out_dtype),
        compiler_params=_params("parallel"),
        name="linear",
    )(*args)


def _linear_t_kernel(xt_ref, w_ref, r_ref, o_ref):
    o_ref[...] = r_ref[...] + _dot_tn(xt_ref[...], w_ref[...])


def _linear_t(x_t, w, residual, tm=512):
    batch, k, seq = x_t.shape
    n = w.shape[1]
    tm = min(tm, seq)
    nt = seq // tm
    return pl.pallas_call(
        _linear_t_kernel, grid=(batch, nt),
        in_specs=[
            pl.BlockSpec((pl.Squeezed(), k, tm), lambda b, i: (b, 0, i)),
            _resident((k, n)),
            pl.BlockSpec((tm, n), lambda b, i: (b * nt + i, 0)),
        ],
        out_specs=pl.BlockSpec((tm, n), lambda b, i: (b * nt + i, 0)),
        out_shape=jax.ShapeDtypeStruct((batch * seq, n), F32),
        compiler_params=_params("parallel", "parallel"),
        name="linear_t",
    )(x_t, w.astype(BF16), residual)


def _ffn_kernel(*refs, has_final, mixer_out):
    refs = list(refs)
    h_ref, g_ref, wg_ref, wu_ref, wd_ref = refs[:5]
    refs = refs[5:]
    fg_ref = refs.pop(0) if has_final else None
    mo_ref, wo_ref = (refs.pop(0), refs.pop(0)) if mixer_out else (None, None)
    o_ref = refs[-1]
    x = h_ref[...]
    if mixer_out == "rows":
        x = x + _dot(mo_ref[...], wo_ref[...])
    elif mixer_out == "features":
        x = x + _dot_tn(mo_ref[...], wo_ref[...])
    xn = _rms(x, g_ref[...]).astype(BF16)
    a = (_silu(_dot(xn, wg_ref[...])) * _dot(xn, wu_ref[...])).astype(BF16)
    y = x + _dot(a, wd_ref[...])
    if has_final:
        y = _rms(y, fg_ref[...])
    o_ref[...] = y


def _ffn(h, gain, w_gate, w_up, w_down, final_gain=None, mixer=None, tm=512):
    m, d = h.shape
    ff = w_gate.shape[1]
    tm = min(tm, m)
    if mixer is not None and mixer[0].ndim == 3:
        tm = min(tm, mixer[0].shape[2])
    in_specs = [
        pl.BlockSpec((tm, d), lambda i: (i, 0)),
        _resident((1, d)), _resident((d, ff)), _resident((d, ff)), _resident((ff, d)),
    ]
    args = [h, _row(gain), w_gate.astype(BF16), w_up.astype(BF16), w_down.astype(BF16)]
    if final_gain is not None:
        in_specs.append(_resident((1, d)))
        args.append(_row(final_gain))
    mixer_out = None
    if mixer is not None:
        mo, w_out = mixer
        if mo.ndim == 2:
            mixer_out = "rows"
            in_specs.append(pl.BlockSpec((tm, mo.shape[1]), lambda i: (i, 0)))
        else:
            mixer_out = "features"
            nt = mo.shape[2] // tm
            in_specs.append(pl.BlockSpec((pl.Squeezed(), mo.shape[1], tm), lambda i: (i // nt, 0, i % nt)))
        in_specs.append(_resident(w_out.shape))
        args += [mo, w_out.astype(BF16)]
    return pl.pallas_call(
        functools.partial(_ffn_kernel, has_final=final_gain is not None, mixer_out=mixer_out),
        grid=(m // tm,), in_specs=in_specs,
        out_specs=pl.BlockSpec((tm, d), lambda i: (i, 0)),
        out_shape=jax.ShapeDtypeStruct((m, d), F32),
        compiler_params=_params("parallel"),
        name="ffn",
    )(*args)


def _gla_decay_kernel(h_ref, g_ref, w1_ref, w2_ref, b_ref, o_ref):
    xn = _rms(h_ref[...], g_ref[...]).astype(BF16)
    lr = _dot(xn, w1_ref[...]).astype(BF16)
    x = _dot(lr, w2_ref[...]) + b_ref[...]
    o_ref[...] = _log_sigmoid(x) * (1.0 / GLA_TAU)


def _gla_decay(h, gain, w_lr, w_gate2, b_gate, tm=512):
    m, d = h.shape
    rank = w_lr.shape[1]
    n = w_gate2.shape[1]
    tm = min(tm, m)
    w1 = jnp.zeros((d, LANES), BF16).at[:, :rank].set(w_lr.astype(BF16))
    w2 = jnp.zeros((LANES, n), BF16).at[:rank, :].set(w_gate2.astype(BF16))
    return pl.pallas_call(
        _gla_decay_kernel, grid=(m // tm,),
        in_specs=[
            pl.BlockSpec((tm, d), lambda i: (i, 0)),
            pl.BlockSpec((1, d), lambda i: (0, 0)),
            pl.BlockSpec((d, LANES), lambda i: (0, 0)),
            pl.BlockSpec((LANES, n), lambda i: (0, 0)),
            pl.BlockSpec((1, n), lambda i: (0, 0)),
        ],
        out_specs=pl.BlockSpec((tm, n), lambda i: (i, 0)),
        out_shape=jax.ShapeDtypeStruct((m, n), F32),
        compiler_params=_params("parallel"),
        name="gla_decay",
    )(h, _row(gain), w1, w2, _row(b_gate))


def _gla_core_kernel(q_ref, k_ref, v_ref, go_ref, la_ref, gn_ref, o_ref, st_ref, *, n_chunks, scale):
    c_len, sub = GLA_CHUNK, GLA_SUB
    n_sub = c_len // sub
    tr = n_chunks * c_len

    @pl.when(pl.program_id(2) == 0)
    def _():
        st_ref[...] = jnp.zeros_like(st_ref)

    r_io = lax.broadcasted_iota(jnp.int32, (tr, tr), 0)
    c_io = lax.broadcasted_iota(jnp.int32, (tr, tr), 1)
    tril = jnp.where((r_io >= c_io) & (r_io // c_len == c_io // c_len), 1.0, 0.0).astype(BF16)
    hi, lo = _split_bf16(la_ref[...])
    bcum = _dot(tril, hi) + _dot(tril, lo)
    q = q_ref[...] * scale
    k = k_ref[...]
    v = v_ref[...].astype(BF16)
    dk = q.shape[1]
    qe = (q * jnp.exp(bcum)).astype(BF16)

    causal = [lax.broadcasted_iota(jnp.int32, (sub, (i + 1) * sub), 1)
              <= lax.broadcasted_iota(jnp.int32, (sub, (i + 1) * sub), 0) + i * sub for i in range(n_sub)]
    qs, ks, kdecs, blasts = [], [], [], []
    for c in range(n_chunks):
        bc = bcum[c * c_len:(c + 1) * c_len]
        qc = q[c * c_len:(c + 1) * c_len]
        kc = k[c * c_len:(c + 1) * c_len]
        for i in range(n_sub):
            r0, r1 = i * sub, (i + 1) * sub
            base = bc[r0 - 1:r0, :] if i else jnp.zeros((1, dk), F32)
            qs.append((qc[r0:r1] * jnp.exp(bc[r0:r1] - base)).astype(BF16))
            ks.append((kc[:r1] * jnp.exp(base - bc[:r1])).astype(BF16))
        blast = bc[c_len - 1:c_len, :]
        blasts.append(blast)
        kdecs.append((kc * jnp.exp(blast - bc)).astype(BF16))
    scores = [_dot_nt(qi, ki) for qi, ki in zip(qs, ks)]
    scores = [jnp.where(causal[n % n_sub], a, 0.0).astype(BF16) for n, a in enumerate(scores)]
    intra = [_dot(a, v[(n // n_sub) * c_len:(n // n_sub) * c_len + (n % n_sub + 1) * sub])
             for n, a in enumerate(scores)]
    updates = [_dot_tn(v[c * c_len:(c + 1) * c_len], kdecs[c]) for c in range(n_chunks)]
    st = st_ref[...]
    inter = []
    for c in range(n_chunks):
        inter.append(_dot_nt(qe[c * c_len:(c + 1) * c_len], st.astype(BF16)))
        st = st * jnp.exp(blasts[c]) + updates[c]
    st_ref[...] = st
    o = jnp.concatenate(inter, axis=0) + jnp.concatenate(intra, axis=0)
    o_ref[...] = (_rms(o, gn_ref[...]) * _silu(go_ref[...])).astype(o_ref.dtype)


def _gla_core(proj, log_a, g_norm, batch, seq, tr=512):
    t = proj.shape[0]
    h_n = GLA_HEADS
    dk = log_a.shape[1] // h_n
    dv = (proj.shape[1] - 2 * h_n * dk) // (2 * h_n)
    tr = min(tr, seq)
    nt = seq // tr
    kb, vb, gb = h_n, (2 * h_n * dk) // dv, (2 * h_n * dk) // dv + h_n
    kern = functools.partial(_gla_core_kernel, n_chunks=tr // GLA_CHUNK, scale=dk ** -0.5)
    return pl.pallas_call(
        kern, grid=(batch, h_n, nt),
        in_specs=[
            pl.BlockSpec((tr, dk), lambda b, h, i: (b * nt + i, h)),
            pl.BlockSpec((tr, dk), lambda b, h, i: (b * nt + i, kb + h)),
            pl.BlockSpec((tr, dv), lambda b, h, i: (b * nt + i, vb + h)),
            pl.BlockSpec((tr, dv), lambda b, h, i: (b * nt + i, gb + h)),
            pl.BlockSpec((tr, dk), lambda b, h, i: (b * nt + i, h)),
            pl.BlockSpec((1, dv), lambda b, h, i: (0, 0)),
        ],
        out_specs=pl.BlockSpec((tr, dv), lambda b, h, i: (b * nt + i, h)),
        out_shape=jax.ShapeDtypeStruct((t, h_n * dv), BF16),
        scratch_shapes=[pltpu.VMEM((dv, dk), F32)],
        compiler_params=_params("parallel", "parallel", "arbitrary"),
        name="gla_core",
    )(proj, proj, proj, proj, log_a, _row(g_norm))


def _gla_mixer(h, gain, w_in, w_gate2, b_gate, g_norm, w_out, batch, seq, defer_out=False):
    n_main = w_in.shape[1] - w_gate2.shape[0]
    proj = _linear(h, w_in[:, :n_main], gain=gain)
    log_a = _gla_decay(h, gain, w_in[:, n_main:], w_gate2, b_gate)
    o = _gla_core(proj, log_a, g_norm, batch, seq)
    if defer_out:
        return o, w_out
    return _linear(o, w_out, residual=h)


def _gmlp_sgu_kernel(u_ref, v_ref, lg_ref, lb_ref, ws_ref, bs_ref, wo_ref, h_ref, o_ref, p_ref, *, n_chunks):
    c_len = GMLP_CHUNK
    groups = ws_ref.shape[0]
    e = v_ref.shape[1]
    ge = e // groups
    causal = (lax.broadcasted_iota(jnp.int32, (c_len, c_len), 0)
              >= lax.broadcasted_iota(jnp.int32, (c_len, c_len), 1))
    for c in range(n_chunks):
        rows = slice(c * c_len, (c + 1) * c_len)
        vn = _layer_norm(v_ref[rows, :].astype(F32), lg_ref[...], lb_ref[...]).astype(BF16)
        for g in range(groups):
            cols = slice(g * ge, (g + 1) * ge)
            ws = jnp.where(causal, ws_ref[g], 0.0).astype(BF16)
            sv = _dot(ws, vn[:, cols]) + bs_ref[:, cols]
            p_ref[rows, cols] = (u_ref[rows, cols].astype(F32) * sv).astype(BF16)
    o_ref[...] = h_ref[...] + _dot(p_ref[...], wo_ref[...])


def _gmlp_mixer(h, gain, w_in, b_in, ln_g, ln_b, w_s, b_s, w_out, tm=512):
    m, d = h.shape
    e = w_out.shape[0]
    groups, c_len = w_s.shape[0], w_s.shape[1]
    z = _linear(h, w_in, gain=gain, bias=b_in, act="gelu", out_dtype=BF16)
    tm = min(tm, m)
    bs_full = jnp.repeat(b_s.T, e // groups, axis=1)
    return pl.pallas_call(
        functools.partial(_gmlp_sgu_kernel, n_chunks=tm // c_len),
        grid=(m // tm,),
        in_specs=[
            pl.BlockSpec((tm, e), lambda i: (i, 0)),
            pl.BlockSpec((tm, e), lambda i: (i, 1)),
            pl.BlockSpec((1, e), lambda i: (0, 0)),
            pl.BlockSpec((1, e), lambda i: (0, 0)),
            pl.BlockSpec((groups, c_len, c_len), lambda i: (0, 0, 0)),
            pl.BlockSpec((c_len, e), lambda i: (0, 0)),
            pl.BlockSpec((e, d), lambda i: (0, 0)),
            pl.BlockSpec((tm, d), lambda i: (i, 0)),
        ],
        out_specs=pl.BlockSpec((tm, d), lambda i: (i, 0)),
        out_shape=jax.ShapeDtypeStruct((m, d), F32),
        scratch_shapes=[pltpu.VMEM((tm, e), BF16)],
        compiler_params=_params("parallel"),
        name="gmlp_sgu",
    )(z, z, _row(ln_g), _row(ln_b), w_s, bs_full, w_out.astype(BF16), h)


def _conv_kernel(h_ref, halo_ref, g_ref, w1_ref, b1_ref, wdw_ref, bdw_ref, lg_ref, lb_ref, w2_ref, b2_ref,
                 o_ref, y_ref, c_ref, *, tm, rc, lc):
    d = h_ref.shape[1]
    halo = CONV_HALO

    def glu(x):
        ag = _dot(_rms(x, g_ref[...]).astype(BF16), w1_ref[...]) + b1_ref[...]
        return ag[:, :d] * _sigmoid(ag[:, d:])

    first = pl.program_id(1) == 0
    y_ref[0:halo, :] = jnp.where(first, 0.0, glu(halo_ref[...]))
    y_ref[halo:, :] = glu(h_ref[...])
    off = halo - (CONV_WIDTH - 1)
    sl = SUBLANES
    for r in range(tm // rc):
        for lb in range(d // lc):
            lanes = slice(lb * lc, (lb + 1) * lc)
            out = None
            for phase in range(sl):
                rows = rc + (sl if phase else 0)
                part = None
                for j in range(CONV_WIDTH):
                    if (off + j) % sl != phase:
                        continue
                    a = r * rc + (off + j) // sl * sl
                    win = y_ref[a:a + rows, lanes].reshape(rows // sl, sl, lc)
                    term = win * wdw_ref[sl * j:sl * (j + 1), lanes][None]
                    part = term if part is None else part + term
                part = part.reshape(rows, lc)[phase:phase + rc]
                out = part if out is None else out + part
            c_ref[r * rc:(r + 1) * rc, lanes] = out
    yc = _layer_norm(c_ref[...] + bdw_ref[...], lg_ref[...], lb_ref[...])
    o_ref[...] = h_ref[...] + _dot(_silu(yc).astype(BF16), w2_ref[...]) + b2_ref[...]


def _conv_mixer(h, gain, w_pw1, b_pw1, w_dw, b_dw, ln_g, ln_b, w_pw2, b_pw2, batch, seq, tm=512, rc=32, lc=256):
    m, d = h.shape
    tm = min(tm, seq)
    nt = seq // tm
    hb = tm // CONV_HALO
    kern = functools.partial(_conv_kernel, tm=tm, rc=rc, lc=lc)
    return pl.pallas_call(
        kern, grid=(batch, nt),
        in_specs=[
            pl.BlockSpec((tm, d), lambda b, i: (b * nt + i, 0)),
            pl.BlockSpec((CONV_HALO, d), lambda b, i: (jnp.maximum((b * nt + i) * hb - 1, 0), 0)),
            pl.BlockSpec((1, d), lambda b, i: (0, 0)),
            pl.BlockSpec((d, 2 * d), lambda b, i: (0, 0)),
            pl.BlockSpec((1, 2 * d), lambda b, i: (0, 0)),
            pl.BlockSpec((CONV_WIDTH * SUBLANES, d), lambda b, i: (0, 0)),
            pl.BlockSpec((1, d), lambda b, i: (0, 0)),
            pl.BlockSpec((1, d), lambda b, i: (0, 0)),
            pl.BlockSpec((1, d), lambda b, i: (0, 0)),
            pl.BlockSpec((d, d), lambda b, i: (0, 0)),
            pl.BlockSpec((1, d), lambda b, i: (0, 0)),
        ],
        out_specs=pl.BlockSpec((tm, d), lambda b, i: (b * nt + i, 0)),
        out_shape=jax.ShapeDtypeStruct((m, d), F32),
        scratch_shapes=[pltpu.VMEM((tm + CONV_HALO, d), F32), pltpu.VMEM((tm, d), F32)],
        compiler_params=_params("parallel", "arbitrary"),
        name="conv_mixer",
    )(h, h, _row(gain), w_pw1.astype(BF16), _row(b_pw1), jnp.repeat(w_dw, SUBLANES, axis=0),
      _row(b_dw), _row(ln_g), _row(ln_b),
      w_pw2.astype(BF16), _row(b_pw2))


def _nsa_compress_kernel(x_ref, plo_ref, phi_ref, w1a_ref, w1b_ref, w2_ref, w2t_ref, k_ref, vt_ref):
    x = x_ref[...]
    a = _dot((x + plo_ref[...]).astype(BF16), w1a_ref[...])
    b = _dot((x + phi_ref[...]).astype(BF16), w1b_ref[...])
    b_next = jnp.concatenate([b[1:], b[:1]], axis=0)
    hid = _gelu_tanh(a + b_next).astype(BF16)
    k_ref[...] = _dot(hid, w2_ref[...]).astype(k_ref.dtype)
    vt_ref[...] = _dot_nt(w2t_ref[...], hid).astype(vt_ref.dtype)


def _nsa_compress(x2, pe, w1, w2):
    two, groups, batch, ns, kd = x2.shape
    dh = w2.shape[2]
    hid = w1.shape[2]
    plo = pe[:, :NSA_CMP_STRIDE].reshape(two, 1, kd)
    phi = pe[:, NSA_CMP_STRIDE:].reshape(two, 1, kd)
    w1a = w1[:, :kd].astype(BF16)
    w1b = w1[:, kd:].astype(BF16)
    w2b = w2.astype(BF16)
    w2t = jnp.swapaxes(w2, 1, 2).astype(BF16)
    sq = pl.Squeezed()
    return pl.pallas_call(
        _nsa_compress_kernel, grid=(two, batch, groups),
        in_specs=[
            pl.BlockSpec((sq, sq, sq, ns, kd), lambda c, b, g: (c, g, b, 0, 0)),
            pl.BlockSpec((sq, 1, kd), lambda c, b, g: (c, 0, 0)),
            pl.BlockSpec((sq, 1, kd), lambda c, b, g: (c, 0, 0)),
            pl.BlockSpec((sq, kd, hid), lambda c, b, g: (c, 0, 0)),
            pl.BlockSpec((sq, kd, hid), lambda c, b, g: (c, 0, 0)),
            pl.BlockSpec((sq, hid, dh), lambda c, b, g: (c, 0, 0)),
            pl.BlockSpec((sq, dh, hid), lambda c, b, g: (c, 0, 0)),
        ],
        out_specs=[
            pl.BlockSpec((sq, sq, sq, ns, dh), lambda c, b, g: (c, b, g, 0, 0)),
            pl.BlockSpec((sq, sq, sq, dh, ns), lambda c, b, g: (c, b, g, 0, 0)),
        ],
        out_shape=[
            jax.ShapeDtypeStruct((two, batch, groups, ns, dh), BF16),
            jax.ShapeDtypeStruct((two, batch, groups, dh, ns), BF16),
        ],
        compiler_params=_params("parallel", "parallel", "parallel"),
        name="nsa_compress",
    )(x2, plo, phi, w1a, w1b, w2b, w2t)


def _nsa_select_kernel(sl_ref, q_ref, kc_ref, vct_ref, ovt_ref, pool_ref, oc_ref, selb_ref, flag_ref,
                       imp_ref, x_ref, cnt_ref, *, n_sel, top_n):
    qb, dh, sel_blk = NSA_Q_BLOCK, NSA_HEAD_DIM, NSA_SEL_BLOCK
    rep = q_ref.shape[1] // dh
    g_idx = pl.program_id(1)
    i_blk = pl.program_id(2)
    q0 = i_blk * qb
    nc_pad = kc_ref.shape[0]
    ch = min(NSA_CMP_CHUNK, nc_pad)

    q_tile = q_ref[...]
    q = jnp.concatenate([q_tile[:, r * dh:(r + 1) * dh] for r in range(rep)], axis=0)
    slopes = [sl_ref[g_idx * rep + r] for r in range(rep)]
    t_row = q0 + lax.broadcasted_iota(jnp.int32, (1, qb), 1)
    n_last = (q0 + qb - NSA_CMP_BLOCK) // NSA_CMP_STRIDE

    def compressed_branch(n_rows):
        n_col = lax.broadcasted_iota(jnp.int32, (n_rows, 1), 0)
        c_start = n_col * NSA_CMP_STRIDE
        mask = (c_start + (NSA_CMP_BLOCK - 1) <= t_row) & (n_col < nc_pad - 1)
        dist = t_row.astype(F32) - (c_start.astype(F32) + (NSA_CMP_BLOCK - 1) / 2.0)
        s_all = _dot_nt(kc_ref[0:n_rows, :], q)
        p_sum = jnp.zeros((n_rows, qb), F32)
        p_parts = []
        for r in range(rep):
            s = jnp.where(mask, s_all[:, r * qb:(r + 1) * qb] - slopes[r] * dist, NEG_BIG)
            m = jnp.max(s, axis=0, keepdims=True)
            e = jnp.where(mask, jnp.exp(s - m), 0.0)
            l = jnp.sum(e, axis=0, keepdims=True)
            p = e * jnp.where(l > 0.0, 1.0 / l, 0.0)
            p_sum = p_sum + p
            p_parts.append(p.astype(BF16))
        o_c = _dot(vct_ref[:, 0:n_rows], jnp.concatenate(p_parts, axis=1))
        for r in range(rep):
            oc_ref[r * dh:(r + 1) * dh, :] = o_c[:, r * qb:(r + 1) * qb]
        p_hi, p_lo = _split_bf16(p_sum)
        imp_ref[...] = _dot(ovt_ref[:, 0:n_rows], p_hi) + _dot(ovt_ref[:, 0:n_rows], p_lo)

    n_variants = nc_pad // ch
    need = jnp.minimum(n_last // ch, n_variants - 1)
    for v in range(n_variants):
        @pl.when(need == v)
        def _():
            compressed_branch((v + 1) * ch)

    imp = imp_ref[...]

    j_col = lax.broadcasted_iota(jnp.int32, (n_sel, 1), 0)
    cur = t_row // sel_blk
    forced = (j_col == 0) | (j_col == cur) | (j_col == cur - 1)
    valid = j_col * sel_blk <= t_row
    x_ref[...] = jnp.where(forced, jnp.inf, jnp.where(valid, imp, -jnp.inf))
    cnt_ref[...] = jnp.zeros(cnt_ref.shape, jnp.int32)
    n_live = (q0 + qb - 1) // sel_blk + 1
    ic = min(NSA_RANK_CHUNK, n_sel)
    rq = min(NSA_RANK_ROWS, n_sel)
    jj = lax.broadcasted_iota(jnp.int32, (SUBLANES, 1), 0)
    for c0 in range(0, n_sel, ic):
        @pl.when(c0 < n_live)
        def _():
            for r0 in range(0, n_sel, rq):
                @pl.when(r0 < n_live)
                def _():
                    groups = range(r0 // SUBLANES, (r0 + rq) // SUBLANES)
                    xg = {a: x_ref[SUBLANES * a:SUBLANES * (a + 1), :] for a in groups}
                    cnt = {a: cnt_ref[SUBLANES * a:SUBLANES * (a + 1), :] for a in groups}
                    for i in range(c0, c0 + ic):
                        xi = x_ref[i:i + 1, :]
                        for a in groups:
                            if SUBLANES * a > i:
                                cnt[a] = cnt[a] + jnp.where(xi >= xg[a], 1, 0)
                            elif SUBLANES * a + SUBLANES - 1 < i:
                                cnt[a] = cnt[a] + jnp.where(xi > xg[a], 1, 0)
                            else:
                                tie = jnp.where(jj + SUBLANES * a > i, 1, 0)
                                cnt[a] = cnt[a] + jnp.where(xi > xg[a], 1, jnp.where(xi == xg[a], tie, 0))
                    for a in groups:
                        cnt_ref[SUBLANES * a:SUBLANES * (a + 1), :] = cnt[a]
    chosen = valid & (cnt_ref[...] < top_n)
    selb_ref[...] = jnp.where(chosen, 0.0, NEG_BIG)
    per_block = _dot_nt(jnp.ones((SUBLANES, qb), BF16), jnp.where(chosen, 1.0, 0.0).astype(BF16))
    flag_ref[...] = _dot(per_block.astype(BF16), pool_ref[...])


def _nsa_attend_kernel(sl_ref, flag_ref, q_ref, qa_ref, ks_ref, vst_ref, kw_ref, vwt_ref, selb_ref, oc_ref,
                       gt_ref, o_ref, m_ref, l_ref, acc_ref, list_ref):
    qb, dh, kt, sub = NSA_Q_BLOCK, NSA_HEAD_DIM, NSA_KEY_TILE, NSA_SEL_BLOCK
    nb = kt // sub
    rep = q_ref.shape[1] // dh
    g_idx = pl.program_id(1)
    q0 = pl.program_id(2) * qb
    kd = q0 // kt

    q_tile = q_ref[...]
    qp = jnp.concatenate(
        [jnp.concatenate([q_tile[:, r * dh:(r + 1) * dh], jnp.broadcast_to(qa_ref[r:r + 1, :], (qb, dh))], axis=1)
         for r in range(rep)], axis=0)
    slopes = [sl_ref[g_idx * rep + r] for r in range(rep)]
    d_rel = lax.broadcasted_iota(jnp.int32, (kt, qb), 1) - lax.broadcasted_iota(jnp.int32, (kt, qb), 0)

    def update(k_ref, vt_ref, specs, state):
        return absorb(vt_ref, specs, scores(k_ref, specs), state)

    def scores(k_ref, specs):
        return [_dot_nt(k_ref[pl.ds(pl.multiple_of(kb * kt, kt), kt), :], qp) for kb, _ in specs]

    def absorb(vt_ref, specs, raws, state):
        k0s = [pl.multiple_of(kb * kt, kt) for kb, _ in specs]
        offs = [(q0 - k0).astype(F32) for k0 in k0s]
        m_out, l_out, acc_out = [], [], []
        for r in range(rep):
            cols = slice(r * qb, (r + 1) * qb)
            cs = [-slopes[r] * off for off in offs]
            ss = [jnp.concatenate([raw[b * sub:(b + 1) * sub, cols] + bias_fn(b) for b in range(nb)], axis=0)
                  for raw, (_, bias_fn) in zip(raws, specs)]
            m_new = None if state is None else state[0][:, cols]
            for s, c in zip(ss, cs):
                top = jnp.max(s, axis=0, keepdims=True) + c
                m_new = top if m_new is None else jnp.maximum(m_new, top)
            l_new = acc = None
            if state is not None:
                alpha = jnp.exp(state[0][:, cols] - m_new)
                l_new = alpha * state[1][:, cols]
                acc = alpha * state[2][:, cols]
            for s, c, k0 in zip(ss, cs, k0s):
                e = jnp.exp(s - (m_new - c))
                e_sum = jnp.sum(e, axis=0, keepdims=True)
                l_new = e_sum if l_new is None else l_new + e_sum
                pv = _dot(vt_ref[:, pl.ds(k0, kt)], e.astype(BF16))
                acc = pv if acc is None else acc + pv
            m_out.append(m_new)
            l_out.append(l_new)
            acc_out.append(acc)
        return jnp.concatenate(m_out, axis=1), jnp.concatenate(l_out, axis=1), jnp.concatenate(acc_out, axis=1)

    def causal_bias(kb):
        return jnp.where(d_rel + (q0 - kb * kt) >= 0, 0.0, NEG_BIG)

    def sel_spec(kb, causal):
        if causal is None:
            return kb, lambda b: selb_ref[pl.ds(kb * nb + b, 1), :]
        return kb, lambda b: selb_ref[pl.ds(kb * nb + b, 1), :] + causal[b * sub:(b + 1) * sub]

    def win_spec(kb):
        dist = d_rel + (q0 - kb * kt)
        wb = jnp.where((dist >= 0) & (dist < NSA_WINDOW), 0.0, NEG_BIG)
        return kb, lambda b: wb[b * sub:(b + 1) * sub]

    near = jnp.where(kd > 0, kd - 1, 1)
    n_win = -(-NSA_WINDOW // kt) + 1
    first = jnp.clip(kd - (n_win - 1), 0, ks_ref.shape[0] // kt - n_win)
    near_specs = [sel_spec(kd, causal_bias(kd)), sel_spec(near, causal_bias(near))]
    win_specs = [win_spec(first + j) for j in range(n_win)]
    near_raw = scores(ks_ref, near_specs)
    win_raw = scores(kw_ref, win_specs)
    sel_state = absorb(vst_ref, near_specs, near_raw, None)
    _, l_w, acc_w = absorb(vwt_ref, win_specs, win_raw, None)
    o_w = acc_w * (1.0 / l_w)
    m_ref[...], l_ref[...], acc_ref[...] = sel_state

    def compact(kb, n):
        list_ref[n] = kb
        return n + (flag_ref[0, kb] > 0).astype(jnp.int32)

    n_act = lax.fori_loop(0, kd - 1, compact, jnp.int32(0))

    def step(kbs):
        state = update(ks_ref, vst_ref, [sel_spec(kb, None) for kb in kbs],
                       (m_ref[...], l_ref[...], acc_ref[...]))
        m_ref[...], l_ref[...], acc_ref[...] = state

    def pair(i, carry):
        step([list_ref[2 * i], list_ref[2 * i + 1]])
        return carry

    lax.fori_loop(0, n_act // 2, pair, 0)

    @pl.when(n_act % 2 == 1)
    def _():
        step([list_ref[n_act - 1]])

    o_s = acc_ref[...] * (1.0 / l_ref[...])

    for r in range(rep):
        cols = slice(r * qb, (r + 1) * qb)
        rows = slice(r * dh, (r + 1) * dh)
        g_c = gt_ref[3 * r:3 * r + 1, :]
        g_s = gt_ref[3 * r + 1:3 * r + 2, :]
        g_w = gt_ref[3 * r + 2:3 * r + 3, :]
        o_ref[rows, :] = (g_c * oc_ref[rows, :] + g_s * o_s[:, cols] + g_w * o_w[:, cols]).astype(o_ref.dtype)


def _alibi_slopes(n):
    return 2.0 ** (-8.0 * jnp.arange(1, n + 1, dtype=F32) / n)


def _nsa_overlap_t(nc_pad, n_sel):
    c_start = NSA_CMP_STRIDE * jnp.arange(nc_pad, dtype=jnp.int32)
    s_start = NSA_SEL_BLOCK * jnp.arange(n_sel, dtype=jnp.int32)
    ov = jnp.clip(jnp.minimum(c_start[:, None] + NSA_CMP_BLOCK, s_start[None] + NSA_SEL_BLOCK)
                  - jnp.maximum(c_start[:, None], s_start[None]), 0, None).astype(F32) / NSA_CMP_BLOCK
    return ov.T.astype(BF16)


def _nsa_proj_kernel(h_ref, g_ref, wqk_ref, wvt_ref, wc_ref, wgt_ref,
                     q_ref, ks_ref, kw_ref, vst_ref, vwt_ref, c_ref, gt_ref):
    groups, tm, dh2 = ks_ref.shape
    dh = dh2 // 2
    n_q = q_ref.shape[1]
    n_kv = groups * dh
    xn = _rms(h_ref[...], g_ref[...]).astype(BF16)
    y = _dot(xn, wqk_ref[...])
    q_ref[...] = y[:, :n_q].astype(q_ref.dtype)
    row = pl.program_id(0) * tm + lax.broadcasted_iota(jnp.int32, (tm, dh), 0)
    lane = lax.broadcasted_iota(jnp.int32, (tm, dh), 1)
    feat = jnp.where(lane < 3, (row % NSA_KEY_TILE).astype(F32), 0.0).astype(ks_ref.dtype)
    for g in range(groups):
        ks_ref[g, :, 0:dh] = y[:, n_q + g * dh:n_q + (g + 1) * dh].astype(ks_ref.dtype)
        ks_ref[g, :, dh:dh2] = feat
        kw_ref[g, :, 0:dh] = y[:, n_q + n_kv + g * dh:n_q + n_kv + (g + 1) * dh].astype(kw_ref.dtype)
        kw_ref[g, :, dh:dh2] = feat
    vt = _dot_nt(wvt_ref[...], xn)
    vst_ref[...] = vt[:n_kv].astype(vst_ref.dtype)
    vwt_ref[...] = vt[n_kv:].astype(vwt_ref.dtype)
    c = _dot(xn, wc_ref[...])
    for cg in range(2 * groups):
        c_ref[cg] = c[:, cg * dh:(cg + 1) * dh]
    gt_ref[...] = _sigmoid(_dot_nt(wgt_ref[...], xn))


def _nsa_proj(h, gain, w_in, batch, seq, tm=512):
    m, d = h.shape
    heads, groups, dh = NSA_HEADS, NSA_KV_HEADS, NSA_HEAD_DIM
    n_q, n_kv = heads * dh, groups * dh
    n_gate = w_in.shape[1] - (n_q + 6 * n_kv)
    tm = min(tm, seq)
    nt = seq // tm
    kv0 = n_q
    w_qk = jnp.concatenate([w_in[:, :n_q] * dh ** -0.5, w_in[:, kv0 + 2 * n_kv:kv0 + 3 * n_kv],
                            w_in[:, kv0 + 4 * n_kv:kv0 + 5 * n_kv]], axis=1).astype(BF16)
    w_vt = jnp.concatenate([w_in[:, kv0 + 3 * n_kv:kv0 + 4 * n_kv],
                            w_in[:, kv0 + 5 * n_kv:kv0 + 6 * n_kv]], axis=1).T.astype(BF16)
    w_c = w_in[:, kv0:kv0 + 2 * n_kv].astype(BF16)
    w_gt = w_in[:, kv0 + 6 * n_kv:].T.astype(BF16)
    sq = pl.Squeezed()
    return pl.pallas_call(
        _nsa_proj_kernel, grid=(m // tm,),
        in_specs=[
            pl.BlockSpec((tm, d), lambda i: (i, 0)),
            _resident((1, d)), _resident(w_qk.shape), _resident(w_vt.shape), _resident(w_c.shape),
            _resident(w_gt.shape),
        ],
        out_specs=[
            pl.BlockSpec((tm, n_q), lambda i: (i, 0)),
            pl.BlockSpec((groups, tm, 2 * dh), lambda i: (0, i, 0)),
            pl.BlockSpec((groups, tm, 2 * dh), lambda i: (0, i, 0)),
            pl.BlockSpec((sq, n_kv, tm), lambda i: (i // nt, 0, i % nt)),
            pl.BlockSpec((sq, n_kv, tm), lambda i: (i // nt, 0, i % nt)),
            pl.BlockSpec((2 * groups, tm, dh), lambda i: (0, i, 0)),
            pl.BlockSpec((sq, n_gate, tm), lambda i: (i // nt, 0, i % nt)),
        ],
        out_shape=[
            jax.ShapeDtypeStruct((m, n_q), BF16),
            jax.ShapeDtypeStruct((groups, m, 2 * dh), BF16),
            jax.ShapeDtypeStruct((groups, m, 2 * dh), BF16),
            jax.ShapeDtypeStruct((batch, n_kv, seq), BF16),
            jax.ShapeDtypeStruct((batch, n_kv, seq), BF16),
            jax.ShapeDtypeStruct((2 * groups, m, dh), F32),
            jax.ShapeDtypeStruct((batch, n_gate, seq), F32),
        ],
        compiler_params=_params("parallel"),
        name="nsa_proj",
    )(h, _row(gain), w_qk, w_vt, w_c, w_gt)


def _nsa_mixer(h, gain, w_in, pe_k, w_ck1, w_ck2, pe_v, w_cv1, w_cv2, w_out, batch, seq, defer_out=False):
    m, d = h.shape
    heads, groups, dh = NSA_HEADS, NSA_KV_HEADS, NSA_HEAD_DIM
    rep = heads // groups
    qb = NSA_Q_BLOCK
    n_q = heads * dh
    n_kv = groups * dh
    n_sel = seq // NSA_SEL_BLOCK
    nc_pad = seq // NSA_CMP_STRIDE
    top_n = min(NSA_TOP_N, n_sel)

    att, k_sel_aug, k_win_aug, v_sel_t, v_win_t, cmp_src, gates_t = _nsa_proj(h, gain, w_in, batch, seq)
    v_sel_t = v_sel_t.reshape(batch, groups, dh, seq)
    v_win_t = v_win_t.reshape(batch, groups, dh, seq)
    gates_t = gates_t.reshape(batch, groups, rep * 3, seq)
    x2 = cmp_src.reshape(2, groups, batch, nc_pad, NSA_CMP_STRIDE * dh)
    k_cmp_both, v_cmp_t_both = _nsa_compress(
        x2, jnp.stack([pe_k, pe_v]), jnp.stack([w_ck1, w_cv1]), jnp.stack([w_ck2, w_cv2]))
    k_cmp, v_cmp_t = k_cmp_both[0], v_cmp_t_both[1]

    sq = pl.Squeezed()
    nq = seq // qb
    kt = NSA_KEY_TILE
    n_tiles = seq // kt
    slopes = _alibi_slopes(heads)
    pool = (jnp.arange(n_sel)[:, None] // (kt // NSA_SEL_BLOCK) == jnp.arange(LANES)[None, :]).astype(BF16)
    o_c, sel_bias, tile_hits = pl.pallas_call(
        functools.partial(_nsa_select_kernel, n_sel=n_sel, top_n=top_n),
        grid=(batch, groups, nq),
        in_specs=[
            pl.BlockSpec(memory_space=pltpu.SMEM),
            pl.BlockSpec((qb, rep * dh), lambda b, g, i: (b * nq + i, g)),
            pl.BlockSpec((sq, sq, nc_pad, dh), lambda b, g, i: (b, g, 0, 0)),
            pl.BlockSpec((sq, sq, dh, nc_pad), lambda b, g, i: (b, g, 0, 0)),
            pl.BlockSpec((n_sel, nc_pad), lambda b, g, i: (0, 0)),
            pl.BlockSpec((n_sel, LANES), lambda b, g, i: (0, 0)),
        ],
        out_specs=[
            pl.BlockSpec((sq, rep * dh, qb), lambda b, g, i: (b, g, i)),
            pl.BlockSpec((sq, sq, n_sel, qb), lambda b, g, i: (b, g, 0, i)),
            pl.BlockSpec((sq, sq, sq, 8, LANES), lambda b, g, i: (b, g, i, 0, 0)),
        ],
        out_shape=[
            jax.ShapeDtypeStruct((batch, heads * dh, seq), F32),
            jax.ShapeDtypeStruct((batch, groups, n_sel, seq), F32),
            jax.ShapeDtypeStruct((batch, groups, nq, 8, LANES), F32),
        ],
        scratch_shapes=[pltpu.VMEM((n_sel, qb), F32), pltpu.VMEM((n_sel, qb), F32),
                        pltpu.VMEM((n_sel, qb), jnp.int32)],
        compiler_params=_params("parallel", "parallel", "parallel"),
        name="nsa_select",
    )(slopes, att, k_cmp, v_cmp_t, _nsa_overlap_t(nc_pad, n_sel), pool)
    tile_flags = (tile_hits[:, :, :, 0, :n_tiles] > 0.5).astype(jnp.int32).reshape(batch * groups * nq, 1, n_tiles)

    s_hi = slopes.astype(BF16)
    s_mid = (slopes - s_hi.astype(F32)).astype(BF16)
    s_lo = (slopes - s_hi.astype(F32) - s_mid.astype(F32)).astype(BF16)
    q_aug = jnp.zeros((heads, dh), BF16).at[:, 0].set(s_hi).at[:, 1].set(s_mid).at[:, 2].set(s_lo)
    q_aug = q_aug.reshape(groups, rep, dh)

    o_t = pl.pallas_call(
        _nsa_attend_kernel, grid=(batch, groups, nq),
        in_specs=[
            pl.BlockSpec(memory_space=pltpu.SMEM),
            pl.BlockSpec((sq, 1, n_tiles), lambda b, g, i: ((b * groups + g) * nq + i, 0, 0),
                         memory_space=pltpu.SMEM),
            pl.BlockSpec((qb, rep * dh), lambda b, g, i: (b * nq + i, g)),
            pl.BlockSpec((sq, rep, dh), lambda b, g, i: (g, 0, 0)),
            pl.BlockSpec((sq, seq, 2 * dh), lambda b, g, i: (g, b, 0)),
            pl.BlockSpec((sq, sq, dh, seq), lambda b, g, i: (b, g, 0, 0)),
            pl.BlockSpec((sq, seq, 2 * dh), lambda b, g, i: (g, b, 0)),
            pl.BlockSpec((sq, sq, dh, seq), lambda b, g, i: (b, g, 0, 0)),
            pl.BlockSpec((sq, sq, n_sel, qb), lambda b, g, i: (b, g, 0, i)),
            pl.BlockSpec((sq, rep * dh, qb), lambda b, g, i: (b, g, i)),
            pl.BlockSpec((sq, sq, rep * 3, qb), lambda b, g, i: (b, g, 0, i)),
        ],
        out_specs=pl.BlockSpec((sq, rep * dh, qb), lambda b, g, i: (b, g, i)),
        out_shape=jax.ShapeDtypeStruct((batch, heads * dh, seq), BF16),
        scratch_shapes=[pltpu.VMEM((1, rep * qb), F32), pltpu.VMEM((1, rep * qb), F32),
                        pltpu.VMEM((dh, rep * qb), F32), pltpu.SMEM((n_tiles,), jnp.int32)],
        compiler_params=_params("parallel", "parallel", "arbitrary"),
        name="nsa_attend",
    )(slopes, tile_flags, att, q_aug, k_sel_aug, v_sel_t, k_win_aug, v_win_t, sel_bias, o_c, gates_t)
    if defer_out:
        return o_t, w_out
    return _linear_t(o_t, w_out, h)


def kernel(x, norm_mix, norm_ffn, norm_final, w_ffn_gate, w_ffn_up, w_ffn_down, a_w_in, a_w_gate2, a_b_gate, a_g_norm, a_w_out, b_w_in, b_b_in, b_ln_g, b_ln_b, b_w_s, b_b_s, b_w_out, c_w_pw1, c_b_pw1, c_w_dw, c_b_dw, c_ln_g, c_ln_b, c_w_pw2, c_b_pw2, d_w_in, d_pe_k, d_w_ck1, d_w_ck2, d_pe_v, d_w_cv1, d_w_cv2, d_w_out):
    batch, seq, d = x.shape
    depth = norm_mix.shape[0]
    h = x.reshape(batch * seq, d)
    for i in range(depth):
        m, j = i % 4, i // 4
        pending = None
        if m == 0:
            pending = _gla_mixer(h, norm_mix[i], a_w_in[j], a_w_gate2[j], a_b_gate[j], a_g_norm[j], a_w_out[j],
                                 batch, seq, defer_out=True)
        elif m == 1:
            h = _gmlp_mixer(h, norm_mix[i], b_w_in[j], b_b_in[j], b_ln_g[j], b_ln_b[j], b_w_s[j], b_b_s[j],
                            b_w_out[j])
        elif m == 2:
            h = _conv_mixer(h, norm_mix[i], c_w_pw1[j], c_b_pw1[j], c_w_dw[j], c_b_dw[j], c_ln_g[j],
                            c_ln_b[j], c_w_pw2[j], c_b_pw2[j], batch, seq)
        else:
            pending = _nsa_mixer(h, norm_mix[i], d_w_in[j], d_pe_k[j], d_w_ck1[j], d_w_ck2[j], d_pe_v[j],
                                 d_w_cv1[j], d_w_cv2[j], d_w_out[j], batch, seq, defer_out=True)
        final = norm_final if i == depth - 1 else None
        h = _ffn(h, norm_ffn[i], w_ffn_gate[i], w_ffn_up[i], w_ffn_down[i], final_gain=final, mixer=pending)
    return h.reshape(batch, seq, d)
```

```python
import functools

import jax
import jax.numpy as jnp
from jax import lax
from jax.experimental import pallas as pl
from jax.experimental.pallas import tpu as pltpu

F32 = jnp.float32
BF16 = jnp.bfloat16

EPS = 1e-6
V7X_VMEM_BYTES = 64 * 1024 * 1024
VMEM_LIMIT_BYTES = V7X_VMEM_BYTES - 8 * 1024 * 1024
LANES = 128
SUBLANES = 8

GLA_HEADS = 4
GLA_TAU = 16.0
GLA_CHUNK = 64
GLA_SUB = 16
GMLP_GROUPS = 8
GMLP_CHUNK = 128
CONV_WIDTH = 31
CONV_HALO = 32
NSA_HEADS = 16
NSA_KV_HEADS = 4
NSA_HEAD_DIM = 64
NSA_CMP_BLOCK = 32
NSA_CMP_STRIDE = 16
NSA_SEL_BLOCK = 64
NSA_TOP_N = 16
NSA_WINDOW = 512
NSA_Q_BLOCK = 128
NSA_KEY_TILE = 256
NSA_CMP_CHUNK = 128
NSA_RANK_CHUNK = 32
NSA_RANK_ROWS = 64
NEG_BIG = -1e30

NT_DIMS = (((1,), (1,)), ((), ()))
TN_DIMS = (((0,), (0,)), ((), ()))


def _params(*sem):
    return pltpu.CompilerParams(dimension_semantics=sem, vmem_limit_bytes=VMEM_LIMIT_BYTES)


def _dot(a, b):
    return jnp.dot(a, b, preferred_element_type=F32)


def _dot_nt(a, b):
    return lax.dot_general(a, b, NT_DIMS, preferred_element_type=F32)


def _dot_tn(a, b):
    return lax.dot_general(a, b, TN_DIMS, preferred_element_type=F32)


def _sigmoid(x):
    return 1.0 / (1.0 + jnp.exp(-x))


def _silu(x):
    return x * _sigmoid(x)


def _gelu_tanh(x):
    return 0.5 * x * (1.0 + jnp.tanh(0.7978845608028654 * (x + 0.044715 * (x * x * x))))


def _log_sigmoid(x):
    return jnp.minimum(x, 0.0) - jnp.log(1.0 + jnp.exp(-jnp.abs(x)))


def _rms(x, gain):
    ms = jnp.mean(x * x, axis=-1, keepdims=True)
    return x * lax.rsqrt(ms + EPS) * gain


def _layer_norm(x, gain, bias):
    mu = jnp.mean(x, axis=-1, keepdims=True)
    xc = x - mu
    var = jnp.mean(xc * xc, axis=-1, keepdims=True)
    return xc * lax.rsqrt(var + EPS) * gain + bias


def _split_bf16(x):
    hi = x.astype(BF16)
    lo = (x - hi.astype(F32)).astype(BF16)
    return hi, lo


def _row(v):
    return v.reshape(1, -1)


def _linear_kernel(*refs, has_gain, has_bias, has_res, act):
    refs = list(refs)
    x_ref = refs.pop(0)
    g_ref = refs.pop(0) if has_gain else None
    w_ref = refs.pop(0)
    b_ref = refs.pop(0) if has_bias else None
    r_ref = refs.pop(0) if has_res else None
    o_ref = refs.pop(0)

    if has_gain:
        xn = _rms(x_ref[...].astype(F32), g_ref[...]).astype(BF16)
    else:
        xn = x_ref[...].astype(BF16)
    y = _dot(xn, w_ref[...])
    if has_bias:
        y = y + b_ref[...]
    if act == "gelu":
        y = _gelu_tanh(y)
    elif act == "sigmoid":
        y = _sigmoid(y)
    if has_res:
        y = y + r_ref[...]
    o_ref[...] = y.astype(o_ref.dtype)


def _resident(shape):
    return pl.BlockSpec(shape, lambda *_: (0,) * len(shape), pipeline_mode=pl.Buffered(1))


def _linear(x, w, *, gain=None, bias=None, residual=None, act=None, out_dtype=F32, tm=512):
    m, k = x.shape
    n = w.shape[1]
    tm = min(tm, m)
    in_specs = [pl.BlockSpec((tm, k), lambda i: (i, 0))]
    args = [x]
    if gain is not None:
        in_specs.append(_resident((1, k)))
        args.append(_row(gain))
    in_specs.append(_resident((k, n)))
    args.append(w.astype(BF16))
    if bias is not None:
        in_specs.append(_resident((1, n)))
        args.append(_row(bias))
    if residual is not None:
        in_specs.append(pl.BlockSpec((tm, n), lambda i: (i, 0)))
        args.append(residual)
    kern = functools.partial(_linear_kernel, has_gain=gain is not None, has_bias=bias is not None,
                             has_res=residual is not None, act=act)
    return pl.pallas_call(
        kern, grid=(m // tm,), in_specs=in_specs,
        out_specs=pl.BlockSpec((tm, n), lambda i: (i, 0)),
        out_shape=jax.ShapeDtypeStruct((m, n), out_dtype),
        compiler_params=_params("parallel"),
        name="linear",
    )(*args)


def _linear_t_kernel(xt_ref, w_ref, r_ref, o_ref):
    o_ref[...] = r_ref[...] + _dot_tn(xt_ref[...], w_ref[...])


def _linear_t(x_t, w, residual, tm=512):
    batch, k, seq = x_t.shape
    n = w.shape[1]
    tm = min(tm, seq)
    nt = seq // tm
    return pl.pallas_call(
        _linear_t_kernel, grid=(batch, nt),
        in_specs=[
            pl.BlockSpec((pl.Squeezed(), k, tm), lambda b, i: (b, 0, i)),
            _resident((k, n)),
            pl.BlockSpec((tm, n), lambda b, i: (b * nt + i, 0)),
        ],
        out_specs=pl.BlockSpec((tm, n), lambda b, i: (b * nt + i, 0)),
        out_shape=jax.ShapeDtypeStruct((batch * seq, n), F32),
        compiler_params=_params("parallel", "parallel"),
        name="linear_t",
    )(x_t, w.astype(BF16), residual)


def _ffn_kernel(*refs, has_final, mixer_out):
    refs = list(refs)
    h_ref, g_ref, wg_ref, wu_ref, wd_ref = refs[:5]
    refs = refs[5:]
    fg_ref = refs.pop(0) if has_final else None
    mo_ref, wo_ref = (refs.pop(0), refs.pop(0)) if mixer_out else (None, None)
    o_ref = refs[-1]
    x = h_ref[...]
    if mixer_out == "rows":
        x = x + _dot(mo_ref[...], wo_ref[...])
    elif mixer_out == "features":
        x = x + _dot_tn(mo_ref[...], wo_ref[...])
    xn = _rms(x, g_ref[...]).astype(BF16)
    a = (_silu(_dot(xn, wg_ref[...])) * _dot(xn, wu_ref[...])).astype(BF16)
    y = x + _dot(a, wd_ref[...])
    if has_final:
        y = _rms(y, fg_ref[...])
    o_ref[...] = y


def _ffn(h, gain, w_gate, w_up, w_down, final_gain=None, mixer=None, layer=None, tm=512):
    m, d = h.shape
    ff = w_gate.shape[-1]
    tm = min(tm, m)
    if mixer is not None and mixer[0].ndim == 3:
        tm = min(tm, mixer[0].shape[2])

    def weight(shape):
        if layer is None:
            return _resident(shape)
        return pl.BlockSpec((pl.Squeezed(),) + shape, lambda *_: (layer, 0, 0), pipeline_mode=pl.Buffered(1))

    in_specs = [
        pl.BlockSpec((tm, d), lambda i: (i, 0)),
        _resident((1, d)), weight((d, ff)), weight((d, ff)), weight((ff, d)),
    ]
    args = [h, _row(gain), w_gate.astype(BF16), w_up.astype(BF16), w_down.astype(BF16)]
    if final_gain is not None:
        in_specs.append(_resident((1, d)))
        args.append(_row(final_gain))
    mixer_out = None
    if mixer is not None:
        mo, w_out = mixer
        if mo.ndim == 2:
            mixer_out = "rows"
            in_specs.append(pl.BlockSpec((tm, mo.shape[1]), lambda i: (i, 0)))
        else:
            mixer_out = "features"
            nt = mo.shape[2] // tm
            in_specs.append(pl.BlockSpec((pl.Squeezed(), mo.shape[1], tm), lambda i: (i // nt, 0, i % nt)))
        in_specs.append(_resident(w_out.shape))
        args += [mo, w_out.astype(BF16)]
    return pl.pallas_call(
        functools.partial(_ffn_kernel, has_final=final_gain is not None, mixer_out=mixer_out),
        grid=(m // tm,), in_specs=in_specs,
        out_specs=pl.BlockSpec((tm, d), lambda i: (i, 0)),
        out_shape=jax.ShapeDtypeStruct((m, d), F32),
        compiler_params=_params("parallel"),
        name="ffn",
    )(*args)


def _gla_decay_kernel(h_ref, g_ref, w1_ref, w2_ref, b_ref, o_ref):
    xn = _rms(h_ref[...], g_ref[...]).astype(BF16)
    lr = _dot(xn, w1_ref[...]).astype(BF16)
    x = _dot(lr, w2_ref[...]) + b_ref[...]
    o_ref[...] = _log_sigmoid(x) * (1.0 / GLA_TAU)


def _gla_decay(h, gain, w_lr, w_gate2, b_gate, tm=512):
    m, d = h.shape
    rank = w_lr.shape[1]
    n = w_gate2.shape[1]
    tm = min(tm, m)
    w1 = jnp.zeros((d, LANES), BF16).at[:, :rank].set(w_lr.astype(BF16))
    w2 = jnp.zeros((LANES, n), BF16).at[:rank, :].set(w_gate2.astype(BF16))
    return pl.pallas_call(
        _gla_decay_kernel, grid=(m // tm,),
        in_specs=[
            pl.BlockSpec((tm, d), lambda i: (i, 0)),
            pl.BlockSpec((1, d), lambda i: (0, 0)),
            pl.BlockSpec((d, LANES), lambda i: (0, 0)),
            pl.BlockSpec((LANES, n), lambda i: (0, 0)),
            pl.BlockSpec((1, n), lambda i: (0, 0)),
        ],
        out_specs=pl.BlockSpec((tm, n), lambda i: (i, 0)),
        out_shape=jax.ShapeDtypeStruct((m, n), F32),
        compiler_params=_params("parallel"),
        name="gla_decay",
    )(h, _row(gain), w1, w2, _row(b_gate))


def _gla_core_kernel(q_ref, k_ref, v_ref, go_ref, la_ref, gn_ref, o_ref, st_ref, *, n_chunks, scale):
    c_len, sub = GLA_CHUNK, GLA_SUB
    n_sub = c_len // sub
    tr = n_chunks * c_len

    @pl.when(pl.program_id(2) == 0)
    def _():
        st_ref[...] = jnp.zeros_like(st_ref)

    r_io = lax.broadcasted_iota(jnp.int32, (tr, tr), 0)
    c_io = lax.broadcasted_iota(jnp.int32, (tr, tr), 1)
    tril = jnp.where((r_io >= c_io) & (r_io // c_len == c_io // c_len), 1.0, 0.0).astype(BF16)
    hi, lo = _split_bf16(la_ref[...])
    bcum = _dot(tril, hi) + _dot(tril, lo)
    q = q_ref[...] * scale
    k = k_ref[...]
    v = v_ref[...].astype(BF16)
    dk = q.shape[1]
    qe = (q * jnp.exp(bcum)).astype(BF16)

    causal = [lax.broadcasted_iota(jnp.int32, (sub, (i + 1) * sub), 1)
              <= lax.broadcasted_iota(jnp.int32, (sub, (i + 1) * sub), 0) + i * sub for i in range(n_sub)]
    qs, ks, kdecs, blasts = [], [], [], []
    for c in range(n_chunks):
        bc = bcum[c * c_len:(c + 1) * c_len]
        qc = q[c * c_len:(c + 1) * c_len]
        kc = k[c * c_len:(c + 1) * c_len]
        for i in range(n_sub):
            r0, r1 = i * sub, (i + 1) * sub
            base = bc[r0 - 1:r0, :] if i else jnp.zeros((1, dk), F32)
            qs.append((qc[r0:r1] * jnp.exp(bc[r0:r1] - base)).astype(BF16))
            ks.append((kc[:r1] * jnp.exp(base - bc[:r1])).astype(BF16))
        blast = bc[c_len - 1:c_len, :]
        blasts.append(blast)
        kdecs.append((kc * jnp.exp(blast - bc)).astype(BF16))
    scores = [_dot_nt(qi, ki) for qi, ki in zip(qs, ks)]
    scores = [jnp.where(causal[n % n_sub], a, 0.0).astype(BF16) for n, a in enumerate(scores)]
    intra = [_dot(a, v[(n // n_sub) * c_len:(n // n_sub) * c_len + (n % n_sub + 1) * sub])
             for n, a in enumerate(scores)]
    updates = [_dot_tn(v[c * c_len:(c + 1) * c_len], kdecs[c]) for c in range(n_chunks)]
    st = st_ref[...]
    inter = []
    for c in range(n_chunks):
        inter.append(_dot_nt(qe[c * c_len:(c + 1) * c_len], st.astype(BF16)))
        st = st * jnp.exp(blasts[c]) + updates[c]
    st_ref[...] = st
    o = jnp.concatenate(inter, axis=0) + jnp.concatenate(intra, axis=0)
    o_ref[...] = (_rms(o, gn_ref[...]) * _silu(go_ref[...])).astype(o_ref.dtype)


def _gla_core(proj, log_a, g_norm, batch, seq, tr=512):
    t = proj.shape[0]
    h_n = GLA_HEADS
    dk = log_a.shape[1] // h_n
    dv = (proj.shape[1] - 2 * h_n * dk) // (2 * h_n)
    tr = min(tr, seq)
    nt = seq // tr
    kb, vb, gb = h_n, (2 * h_n * dk) // dv, (2 * h_n * dk) // dv + h_n
    kern = functools.partial(_gla_core_kernel, n_chunks=tr // GLA_CHUNK, scale=dk ** -0.5)
    return pl.pallas_call(
        kern, grid=(batch, h_n, nt),
        in_specs=[
            pl.BlockSpec((tr, dk), lambda b, h, i: (b * nt + i, h)),
            pl.BlockSpec((tr, dk), lambda b, h, i: (b * nt + i, kb + h)),
            pl.BlockSpec((tr, dv), lambda b, h, i: (b * nt + i, vb + h)),
            pl.BlockSpec((tr, dv), lambda b, h, i: (b * nt + i, gb + h)),
            pl.BlockSpec((tr, dk), lambda b, h, i: (b * nt + i, h)),
            pl.BlockSpec((1, dv), lambda b, h, i: (0, 0)),
        ],
        out_specs=pl.BlockSpec((tr, dv), lambda b, h, i: (b * nt + i, h)),
        out_shape=jax.ShapeDtypeStruct((t, h_n * dv), BF16),
        scratch_shapes=[pltpu.VMEM((dv, dk), F32)],
        compiler_params=_params("parallel", "parallel", "arbitrary"),
        name="gla_core",
    )(proj, proj, proj, proj, log_a, _row(g_norm))


def _gla_mixer(h, gain, w_in, w_gate2, b_gate, g_norm, w_out, batch, seq, defer_out=False):
    n_main = w_in.shape[1] - w_gate2.shape[0]
    proj = _linear(h, w_in[:, :n_main], gain=gain)
    log_a = _gla_decay(h, gain, w_in[:, n_main:], w_gate2, b_gate)
    o = _gla_core(proj, log_a, g_norm, batch, seq)
    if defer_out:
        return o, w_out
    return _linear(o, w_out, residual=h)


def _gmlp_sgu_kernel(u_ref, v_ref, lg_ref, lb_ref, ws_ref, bs_ref, wo_ref, h_ref, o_ref, p_ref, *, n_chunks):
    c_len = GMLP_CHUNK
    groups = ws_ref.shape[0]
    e = v_ref.shape[1]
    ge = e // groups
    causal = (lax.broadcasted_iota(jnp.int32, (c_len, c_len), 0)
              >= lax.broadcasted_iota(jnp.int32, (c_len, c_len), 1))
    for c in range(n_chunks):
        rows = slice(c * c_len, (c + 1) * c_len)
        vn = _layer_norm(v_ref[rows, :].astype(F32), lg_ref[...], lb_ref[...]).astype(BF16)
        for g in range(groups):
            cols = slice(g * ge, (g + 1) * ge)
            ws = jnp.where(causal, ws_ref[g], 0.0).astype(BF16)
            sv = _dot(ws, vn[:, cols]) + bs_ref[:, cols]
            p_ref[rows, cols] = (u_ref[rows, cols].astype(F32) * sv).astype(BF16)
    o_ref[...] = h_ref[...] + _dot(p_ref[...], wo_ref[...])


def _gmlp_mixer(h, gain, w_in, b_in, ln_g, ln_b, w_s, b_s, w_out, tm=512):
    m, d = h.shape
    e = w_out.shape[0]
    groups, c_len = w_s.shape[0], w_s.shape[1]
    z = _linear(h, w_in, gain=gain, bias=b_in, act="gelu", out_dtype=BF16)
    tm = min(tm, m)
    bs_full = jnp.repeat(b_s.T, e // groups, axis=1)
    return pl.pallas_call(
        functools.partial(_gmlp_sgu_kernel, n_chunks=tm // c_len),
        grid=(m // tm,),
        in_specs=[
            pl.BlockSpec((tm, e), lambda i: (i, 0)),
            pl.BlockSpec((tm, e), lambda i: (i, 1)),
            pl.BlockSpec((1, e), lambda i: (0, 0)),
            pl.BlockSpec((1, e), lambda i: (0, 0)),
            pl.BlockSpec((groups, c_len, c_len), lambda i: (0, 0, 0)),
            pl.BlockSpec((c_len, e), lambda i: (0, 0)),
            pl.BlockSpec((e, d), lambda i: (0, 0)),
            pl.BlockSpec((tm, d), lambda i: (i, 0)),
        ],
        out_specs=pl.BlockSpec((tm, d), lambda i: (i, 0)),
        out_shape=jax.ShapeDtypeStruct((m, d), F32),
        scratch_shapes=[pltpu.VMEM((tm, e), BF16)],
        compiler_params=_params("parallel"),
        name="gmlp_sgu",
    )(z, z, _row(ln_g), _row(ln_b), w_s, bs_full, w_out.astype(BF16), h)


def _conv_kernel(h_ref, halo_ref, g_ref, w1_ref, b1_ref, wdw_ref, bdw_ref, lg_ref, lb_ref, w2_ref, b2_ref,
                 o_ref, y_ref, c_ref, *, tm, rc, lc):
    d = h_ref.shape[1]
    halo = CONV_HALO

    def glu(x):
        ag = _dot(_rms(x, g_ref[...]).astype(BF16), w1_ref[...]) + b1_ref[...]
        return ag[:, :d] * _sigmoid(ag[:, d:])

    first = pl.program_id(1) == 0
    y_ref[0:halo, :] = jnp.where(first, 0.0, glu(halo_ref[...]))
    y_ref[halo:, :] = glu(h_ref[...])
    off = halo - (CONV_WIDTH - 1)
    sl = SUBLANES
    for r in range(tm // rc):
        for lb in range(d // lc):
            lanes = slice(lb * lc, (lb + 1) * lc)
            out = None
            for phase in range(sl):
                rows = rc + (sl if phase else 0)
                part = None
                for j in range(CONV_WIDTH):
                    if (off + j) % sl != phase:
                        continue
                    a = r * rc + (off + j) // sl * sl
                    win = y_ref[a:a + rows, lanes].reshape(rows // sl, sl, lc)
                    term = win * wdw_ref[sl * j:sl * (j + 1), lanes][None]
                    part = term if part is None else part + term
                part = part.reshape(rows, lc)[phase:phase + rc]
                out = part if out is None else out + part
            c_ref[r * rc:(r + 1) * rc, lanes] = out
    yc = _layer_norm(c_ref[...] + bdw_ref[...], lg_ref[...], lb_ref[...])
    o_ref[...] = h_ref[...] + _dot(_silu(yc).astype(BF16), w2_ref[...]) + b2_ref[...]


def _conv_mixer(h, gain, w_pw1, b_pw1, w_dw, b_dw, ln_g, ln_b, w_pw2, b_pw2, batch, seq, tm=512, rc=32, lc=256):
    m, d = h.shape
    tm = min(tm, seq)
    nt = seq // tm
    hb = tm // CONV_HALO
    kern = functools.partial(_conv_kernel, tm=tm, rc=rc, lc=lc)
    return pl.pallas_call(
        kern, grid=(batch, nt),
        in_specs=[
            pl.BlockSpec((tm, d), lambda b, i: (b * nt + i, 0)),
            pl.BlockSpec((CONV_HALO, d), lambda b, i: (jnp.maximum((b * nt + i) * hb - 1, 0), 0)),
            pl.BlockSpec((1, d), lambda b, i: (0, 0)),
            pl.BlockSpec((d, 2 * d), lambda b, i: (0, 0)),
            pl.BlockSpec((1, 2 * d), lambda b, i: (0, 0)),
            pl.BlockSpec((CONV_WIDTH * SUBLANES, d), lambda b, i: (0, 0)),
            pl.BlockSpec((1, d), lambda b, i: (0, 0)),
            pl.BlockSpec((1, d), lambda b, i: (0, 0)),
            pl.BlockSpec((1, d), lambda b, i: (0, 0)),
            pl.BlockSpec((d, d), lambda b, i: (0, 0)),
            pl.BlockSpec((1, d), lambda b, i: (0, 0)),
        ],
        out_specs=pl.BlockSpec((tm, d), lambda b, i: (b * nt + i, 0)),
        out_shape=jax.ShapeDtypeStruct((m, d), F32),
        scratch_shapes=[pltpu.VMEM((tm + CONV_HALO, d), F32), pltpu.VMEM((tm, d), F32)],
        compiler_params=_params("parallel", "arbitrary"),
        name="conv_mixer",
    )(h, h, _row(gain), w_pw1.astype(BF16), _row(b_pw1), jnp.repeat(w_dw, SUBLANES, axis=0),
      _row(b_dw), _row(ln_g), _row(ln_b),
      w_pw2.astype(BF16), _row(b_pw2))


def _nsa_compress_kernel(x_ref, pe_ref, w1_ref, w2_ref, w2t_ref, k_ref, vt_ref):
    stride, dh = NSA_CMP_STRIDE, x_ref.shape[1]
    ns = k_ref.shape[0]
    a = b = None
    for p in range(stride):
        xs = x_ref[pl.ds(p, ns, stride=stride), :]
        ta = _dot((xs + pe_ref[p:p + 1, :]).astype(BF16), w1_ref[p * dh:(p + 1) * dh, :])
        tb = _dot((xs + pe_ref[stride + p:stride + p + 1, :]).astype(BF16),
                  w1_ref[(stride + p) * dh:(stride + p + 1) * dh, :])
        a = ta if a is None else a + ta
        b = tb if b is None else b + tb
    b_next = jnp.concatenate([b[1:], b[:1]], axis=0)
    hid = _gelu_tanh(a + b_next).astype(BF16)
    k_ref[...] = _dot(hid, w2_ref[...]).astype(k_ref.dtype)
    vt_ref[...] = _dot_nt(w2t_ref[...], hid).astype(vt_ref.dtype)


def _nsa_compress(x, pe, w1, w2, batch):
    two, groups, t, dh = x.shape
    seq = t // batch
    ns = seq // NSA_CMP_STRIDE
    hid = w1.shape[2]
    w2b = w2.astype(BF16)
    w2t = jnp.swapaxes(w2, 1, 2).astype(BF16)
    sq = pl.Squeezed()
    return pl.pallas_call(
        _nsa_compress_kernel, grid=(two, batch, groups),
        in_specs=[
            pl.BlockSpec((sq, sq, seq, dh), lambda c, b, g: (c, g, b, 0)),
            pl.BlockSpec((sq, NSA_CMP_BLOCK, dh), lambda c, b, g: (c, 0, 0)),
            pl.BlockSpec((sq, NSA_CMP_BLOCK * dh, hid), lambda c, b, g: (c, 0, 0)),
            pl.BlockSpec((sq, hid, dh), lambda c, b, g: (c, 0, 0)),
            pl.BlockSpec((sq, dh, hid), lambda c, b, g: (c, 0, 0)),
        ],
        out_specs=[
            pl.BlockSpec((sq, sq, sq, ns, dh), lambda c, b, g: (c, b, g, 0, 0)),
            pl.BlockSpec((sq, sq, sq, dh, ns), lambda c, b, g: (c, b, g, 0, 0)),
        ],
        out_shape=[
            jax.ShapeDtypeStruct((two, batch, groups, ns, dh), BF16),
            jax.ShapeDtypeStruct((two, batch, groups, dh, ns), BF16),
        ],
        compiler_params=_params("parallel", "parallel", "parallel"),
        name="nsa_compress",
    )(x, pe, w1.astype(BF16), w2b, w2t)


def _nsa_select_kernel(sl_ref, q_ref, kc_ref, vct_ref, ovt_ref, pool_ref, oc_ref, selb_ref, flag_ref,
                       imp_ref, x_ref, cnt_ref, *, n_sel, top_n):
    qb, dh, sel_blk = NSA_Q_BLOCK, NSA_HEAD_DIM, NSA_SEL_BLOCK
    rep = q_ref.shape[1] // dh
    g_idx = pl.program_id(1)
    i_blk = pl.program_id(2)
    q0 = i_blk * qb
    nc_pad = kc_ref.shape[0]
    ch = min(NSA_CMP_CHUNK, nc_pad)

    q_tile = q_ref[...]
    q = jnp.concatenate([q_tile[:, r * dh:(r + 1) * dh] for r in range(rep)], axis=0)
    slopes = [sl_ref[g_idx * rep + r] for r in range(rep)]
    t_row = q0 + lax.broadcasted_iota(jnp.int32, (1, qb), 1)
    n_last = (q0 + qb - NSA_CMP_BLOCK) // NSA_CMP_STRIDE

    def compressed_branch(n_rows):
        n_col = lax.broadcasted_iota(jnp.int32, (n_rows, 1), 0)
        c_start = n_col * NSA_CMP_STRIDE
        mask = (c_start + (NSA_CMP_BLOCK - 1) <= t_row) & (n_col < nc_pad - 1)
        dist = t_row.astype(F32) - (c_start.astype(F32) + (NSA_CMP_BLOCK - 1) / 2.0)
        s_all = _dot_nt(kc_ref[0:n_rows, :], q)
        p_sum = jnp.zeros((n_rows, qb), F32)
        p_parts = []
        for r in range(rep):
            s = jnp.where(mask, s_all[:, r * qb:(r + 1) * qb] - slopes[r] * dist, NEG_BIG)
            m = jnp.max(s, axis=0, keepdims=True)
            e = jnp.where(mask, jnp.exp(s - m), 0.0)
            l = jnp.sum(e, axis=0, keepdims=True)
            p = e * jnp.where(l > 0.0, 1.0 / l, 0.0)
            p_sum = p_sum + p
            p_parts.append(p.astype(BF16))
        o_c = _dot(vct_ref[:, 0:n_rows], jnp.concatenate(p_parts, axis=1))
        for r in range(rep):
            oc_ref[r * dh:(r + 1) * dh, :] = o_c[:, r * qb:(r + 1) * qb]
        p_hi, p_lo = _split_bf16(p_sum)
        imp_ref[...] = _dot(ovt_ref[:, 0:n_rows], p_hi) + _dot(ovt_ref[:, 0:n_rows], p_lo)

    n_variants = nc_pad // ch
    need = jnp.minimum(n_last // ch, n_variants - 1)
    for v in range(n_variants):
        @pl.when(need == v)
        def _():
            compressed_branch((v + 1) * ch)

    imp = imp_ref[...]

    j_col = lax.broadcasted_iota(jnp.int32, (n_sel, 1), 0)
    cur = t_row // sel_blk
    forced = (j_col == 0) | (j_col == cur) | (j_col == cur - 1)
    valid = j_col * sel_blk <= t_row
    x_ref[...] = jnp.where(forced, jnp.inf, jnp.where(valid, imp, -jnp.inf))
    cnt_ref[...] = jnp.zeros(cnt_ref.shape, jnp.int32)
    n_live = (q0 + qb - 1) // sel_blk + 1
    ic = min(NSA_RANK_CHUNK, n_sel)
    rq = min(NSA_RANK_ROWS, n_sel)
    jj = lax.broadcasted_iota(jnp.int32, (SUBLANES, 1), 0)
    for c0 in range(0, n_sel, ic):
        @pl.when(c0 < n_live)
        def _():
            for r0 in range(0, n_sel, rq):
                @pl.when(r0 < n_live)
                def _():
                    groups = range(r0 // SUBLANES, (r0 + rq) // SUBLANES)
                    xg = {a: x_ref[SUBLANES * a:SUBLANES * (a + 1), :] for a in groups}
                    cnt = {a: cnt_ref[SUBLANES * a:SUBLANES * (a + 1), :] for a in groups}
                    for i in range(c0, c0 + ic):
                        xi = x_ref[i:i + 1, :]
                        for a in groups:
                            if SUBLANES * a > i:
                                cnt[a] = cnt[a] + jnp.where(xi >= xg[a], 1, 0)
                            elif SUBLANES * a + SUBLANES - 1 < i:
                                cnt[a] = cnt[a] + jnp.where(xi > xg[a], 1, 0)
                            else:
                                tie = jnp.where(jj + SUBLANES * a > i, 1, 0)
                                cnt[a] = cnt[a] + jnp.where(xi > xg[a], 1, jnp.where(xi == xg[a], tie, 0))
                    for a in groups:
                        cnt_ref[SUBLANES * a:SUBLANES * (a + 1), :] = cnt[a]
    chosen = valid & (cnt_ref[...] < top_n)
    selb_ref[...] = jnp.where(chosen, 0.0, NEG_BIG)
    per_block = _dot_nt(jnp.ones((SUBLANES, qb), BF16), jnp.where(chosen, 1.0, 0.0).astype(BF16))
    flag_ref[...] = _dot(per_block.astype(BF16), pool_ref[...])


def _nsa_attend_kernel(sl_ref, flag_ref, q_ref, qa_ref, ks_ref, vst_ref, kw_ref, vwt_ref, selb_ref, oc_ref,
                       gt_ref, o_ref, m_ref, l_ref, acc_ref, list_ref):
    qb, dh, kt, sub = NSA_Q_BLOCK, NSA_HEAD_DIM, NSA_KEY_TILE, NSA_SEL_BLOCK
    nb = kt // sub
    rep = q_ref.shape[1] // dh
    g_idx = pl.program_id(1)
    q0 = pl.program_id(2) * qb
    kd = q0 // kt

    q_tile = q_ref[...]
    qp = jnp.concatenate(
        [jnp.concatenate([q_tile[:, r * dh:(r + 1) * dh], jnp.broadcast_to(qa_ref[r:r + 1, :], (qb, dh))], axis=1)
         for r in range(rep)], axis=0)
    slopes = [sl_ref[g_idx * rep + r] for r in range(rep)]
    d_rel = lax.broadcasted_iota(jnp.int32, (kt, qb), 1) - lax.broadcasted_iota(jnp.int32, (kt, qb), 0)

    def update(k_ref, vt_ref, specs, state):
        return absorb(vt_ref, specs, scores(k_ref, specs), state)

    def scores(k_ref, specs):
        return [_dot_nt(k_ref[pl.ds(pl.multiple_of(kb * kt, kt), kt), :], qp) for kb, _ in specs]

    def absorb(vt_ref, specs, raws, state):
        k0s = [pl.multiple_of(kb * kt, kt) for kb, _ in specs]
        offs = [(q0 - k0).astype(F32) for k0 in k0s]
        m_out, l_out, acc_out = [], [], []
        for r in range(rep):
            cols = slice(r * qb, (r + 1) * qb)
            cs = [-slopes[r] * off for off in offs]
            ss = [jnp.concatenate([raw[b * sub:(b + 1) * sub, cols] + bias_fn(b) for b in range(nb)], axis=0)
                  for raw, (_, bias_fn) in zip(raws, specs)]
            m_new = None if state is None else state[0][:, cols]
            for s, c in zip(ss, cs):
                top = jnp.max(s, axis=0, keepdims=True) + c
                m_new = top if m_new is None else jnp.maximum(m_new, top)
            l_new = acc = None
            if state is not None:
                alpha = jnp.exp(state[0][:, cols] - m_new)
                l_new = alpha * state[1][:, cols]
                acc = alpha * state[2][:, cols]
            for s, c, k0 in zip(ss, cs, k0s):
                e = jnp.exp(s - (m_new - c))
                e_sum = jnp.sum(e, axis=0, keepdims=True)
                l_new = e_sum if l_new is None else l_new + e_sum
                pv = _dot(vt_ref[:, pl.ds(k0, kt)], e.astype(BF16))
                acc = pv if acc is None else acc + pv
            m_out.append(m_new)
            l_out.append(l_new)
            acc_out.append(acc)
        return jnp.concatenate(m_out, axis=1), jnp.concatenate(l_out, axis=1), jnp.concatenate(acc_out, axis=1)

    def causal_bias(kb):
        return jnp.where(d_rel + (q0 - kb * kt) >= 0, 0.0, NEG_BIG)

    def sel_spec(kb, causal):
        if causal is None:
            return kb, lambda b: selb_ref[pl.ds(kb * nb + b, 1), :]
        return kb, lambda b: selb_ref[pl.ds(kb * nb + b, 1), :] + causal[b * sub:(b + 1) * sub]

    def win_spec(kb):
        dist = d_rel + (q0 - kb * kt)
        wb = jnp.where((dist >= 0) & (dist < NSA_WINDOW), 0.0, NEG_BIG)
        return kb, lambda b: wb[b * sub:(b + 1) * sub]

    near = jnp.where(kd > 0, kd - 1, 1)
    n_win = -(-NSA_WINDOW // kt) + 1
    first = jnp.clip(kd - (n_win - 1), 0, ks_ref.shape[0] // kt - n_win)
    near_specs = [sel_spec(kd, causal_bias(kd)), sel_spec(near, causal_bias(near))]
    win_specs = [win_spec(first + j) for j in range(n_win)]
    near_raw = scores(ks_ref, near_specs)
    win_raw = scores(kw_ref, win_specs)
    sel_state = absorb(vst_ref, near_specs, near_raw, None)
    _, l_w, acc_w = absorb(vwt_ref, win_specs, win_raw, None)
    o_w = acc_w * (1.0 / l_w)
    m_ref[...], l_ref[...], acc_ref[...] = sel_state

    def compact(kb, n):
        list_ref[n] = kb
        return n + (flag_ref[0, kb] > 0).astype(jnp.int32)

    n_act = lax.fori_loop(0, kd - 1, compact, jnp.int32(0))

    def step(kbs):
        state = update(ks_ref, vst_ref, [sel_spec(kb, None) for kb in kbs],
                       (m_ref[...], l_ref[...], acc_ref[...]))
        m_ref[...], l_ref[...], acc_ref[...] = state

    def pair(i, carry):
        step([list_ref[2 * i], list_ref[2 * i + 1]])
        return carry

    lax.fori_loop(0, n_act // 2, pair, 0)

    @pl.when(n_act % 2 == 1)
    def _():
        step([list_ref[n_act - 1]])

    o_s = acc_ref[...] * (1.0 / l_ref[...])

    for r in range(rep):
        cols = slice(r * qb, (r + 1) * qb)
        rows = slice(r * dh, (r + 1) * dh)
        g_c = gt_ref[3 * r:3 * r + 1, :]
        g_s = gt_ref[3 * r + 1:3 * r + 2, :]
        g_w = gt_ref[3 * r + 2:3 * r + 3, :]
        o_ref[rows, :] = (g_c * oc_ref[rows, :] + g_s * o_s[:, cols] + g_w * o_w[:, cols]).astype(o_ref.dtype)


def _alibi_slopes(n):
    return 2.0 ** (-8.0 * jnp.arange(1, n + 1, dtype=F32) / n)


def _nsa_overlap_t(nc_pad, n_sel):
    c_start = NSA_CMP_STRIDE * jnp.arange(nc_pad, dtype=jnp.int32)
    s_start = NSA_SEL_BLOCK * jnp.arange(n_sel, dtype=jnp.int32)
    ov = jnp.clip(jnp.minimum(c_start[:, None] + NSA_CMP_BLOCK, s_start[None] + NSA_SEL_BLOCK)
                  - jnp.maximum(c_start[:, None], s_start[None]), 0, None).astype(F32) / NSA_CMP_BLOCK
    return ov.T.astype(BF16)


def _nsa_proj_kernel(h_ref, g_ref, wqk_ref, wvt_ref, wc_ref, wgt_ref,
                     q_ref, ks_ref, kw_ref, vst_ref, vwt_ref, c_ref, gt_ref):
    groups, tm, dh2 = ks_ref.shape
    dh = dh2 // 2
    n_q = q_ref.shape[1]
    n_kv = groups * dh
    xn = _rms(h_ref[...], g_ref[...]).astype(BF16)
    y = _dot(xn, wqk_ref[...])
    q_ref[...] = y[:, :n_q].astype(q_ref.dtype)
    row = pl.program_id(0) * tm + lax.broadcasted_iota(jnp.int32, (tm, dh), 0)
    lane = lax.broadcasted_iota(jnp.int32, (tm, dh), 1)
    feat = jnp.where(lane < 3, (row % NSA_KEY_TILE).astype(F32), 0.0).astype(ks_ref.dtype)
    for g in range(groups):
        ks_ref[g, :, 0:dh] = y[:, n_q + g * dh:n_q + (g + 1) * dh].astype(ks_ref.dtype)
        ks_ref[g, :, dh:dh2] = feat
        kw_ref[g, :, 0:dh] = y[:, n_q + n_kv + g * dh:n_q + n_kv + (g + 1) * dh].astype(kw_ref.dtype)
        kw_ref[g, :, dh:dh2] = feat
    vt = _dot_nt(wvt_ref[...], xn)
    vst_ref[...] = vt[:n_kv].astype(vst_ref.dtype)
    vwt_ref[...] = vt[n_kv:].astype(vwt_ref.dtype)
    c = _dot(xn, wc_ref[...])
    for cg in range(2 * groups):
        c_ref[cg] = c[:, cg * dh:(cg + 1) * dh]
    gt_ref[...] = _sigmoid(_dot_nt(wgt_ref[...], xn))


def _nsa_proj(h, gain, w_in, batch, seq, tm=512):
    m, d = h.shape
    heads, groups, dh = NSA_HEADS, NSA_KV_HEADS, NSA_HEAD_DIM
    n_q, n_kv = heads * dh, groups * dh
    n_gate = w_in.shape[1] - (n_q + 6 * n_kv)
    tm = min(tm, seq)
    nt = seq // tm
    kv0 = n_q
    w_qk = jnp.concatenate([w_in[:, :n_q] * dh ** -0.5, w_in[:, kv0 + 2 * n_kv:kv0 + 3 * n_kv],
                            w_in[:, kv0 + 4 * n_kv:kv0 + 5 * n_kv]], axis=1).astype(BF16)
    w_vt = jnp.concatenate([w_in[:, kv0 + 3 * n_kv:kv0 + 4 * n_kv],
                            w_in[:, kv0 + 5 * n_kv:kv0 + 6 * n_kv]], axis=1).T.astype(BF16)
    w_c = w_in[:, kv0:kv0 + 2 * n_kv].astype(BF16)
    w_gt = w_in[:, kv0 + 6 * n_kv:].T.astype(BF16)
    sq = pl.Squeezed()
    return pl.pallas_call(
        _nsa_proj_kernel, grid=(m // tm,),
        in_specs=[
            pl.BlockSpec((tm, d), lambda i: (i, 0)),
            _resident((1, d)), _resident(w_qk.shape), _resident(w_vt.shape), _resident(w_c.shape),
            _resident(w_gt.shape),
        ],
        out_specs=[
            pl.BlockSpec((tm, n_q), lambda i: (i, 0)),
            pl.BlockSpec((groups, tm, 2 * dh), lambda i: (0, i, 0)),
            pl.BlockSpec((groups, tm, 2 * dh), lambda i: (0, i, 0)),
            pl.BlockSpec((sq, n_kv, tm), lambda i: (i // nt, 0, i % nt)),
            pl.BlockSpec((sq, n_kv, tm), lambda i: (i // nt, 0, i % nt)),
            pl.BlockSpec((2 * groups, tm, dh), lambda i: (0, i, 0)),
            pl.BlockSpec((sq, n_gate, tm), lambda i: (i // nt, 0, i % nt)),
        ],
        out_shape=[
            jax.ShapeDtypeStruct((m, n_q), BF16),
            jax.ShapeDtypeStruct((groups, m, 2 * dh), BF16),
            jax.ShapeDtypeStruct((groups, m, 2 * dh), BF16),
            jax.ShapeDtypeStruct((batch, n_kv, seq), BF16),
            jax.ShapeDtypeStruct((batch, n_kv, seq), BF16),
            jax.ShapeDtypeStruct((2 * groups, m, dh), F32),
            jax.ShapeDtypeStruct((batch, n_gate, seq), F32),
        ],
        compiler_params=_params("parallel"),
        name="nsa_proj",
    )(h, _row(gain), w_qk, w_vt, w_c, w_gt)


def _nsa_mixer(h, gain, w_in, pe_k, w_ck1, w_ck2, pe_v, w_cv1, w_cv2, w_out, batch, seq, defer_out=False):
    m, d = h.shape
    heads, groups, dh = NSA_HEADS, NSA_KV_HEADS, NSA_HEAD_DIM
    rep = heads // groups
    qb = NSA_Q_BLOCK
    n_q = heads * dh
    n_kv = groups * dh
    n_sel = seq // NSA_SEL_BLOCK
    nc_pad = seq // NSA_CMP_STRIDE
    top_n = min(NSA_TOP_N, n_sel)

    att, k_sel_aug, k_win_aug, v_sel_t, v_win_t, cmp_src, gates_t = _nsa_proj(h, gain, w_in, batch, seq)
    v_sel_t = v_sel_t.reshape(batch, groups, dh, seq)
    v_win_t = v_win_t.reshape(batch, groups, dh, seq)
    gates_t = gates_t.reshape(batch, groups, rep * 3, seq)
    k_cmp_both, v_cmp_t_both = _nsa_compress(
        cmp_src.reshape(2, groups, m, dh), jnp.stack([pe_k, pe_v]), jnp.stack([w_ck1, w_cv1]),
        jnp.stack([w_ck2, w_cv2]), batch)
    k_cmp, v_cmp_t = k_cmp_both[0], v_cmp_t_both[1]

    sq = pl.Squeezed()
    nq = seq // qb
    kt = NSA_KEY_TILE
    n_tiles = seq // kt
    slopes = _alibi_slopes(heads)
    pool = (jnp.arange(n_sel)[:, None] // (kt // NSA_SEL_BLOCK) == jnp.arange(LANES)[None, :]).astype(BF16)
    o_c, sel_bias, tile_hits = pl.pallas_call(
        functools.partial(_nsa_select_kernel, n_sel=n_sel, top_n=top_n),
        grid=(batch, groups, nq),
        in_specs=[
            pl.BlockSpec(memory_space=pltpu.SMEM),
            pl.BlockSpec((qb, rep * dh), lambda b, g, i: (b * nq + i, g)),
            pl.BlockSpec((sq, sq, nc_pad, dh), lambda b, g, i: (b, g, 0, 0)),
            pl.BlockSpec((sq, sq, dh, nc_pad), lambda b, g, i: (b, g, 0, 0)),
            pl.BlockSpec((n_sel, nc_pad), lambda b, g, i: (0, 0)),
            pl.BlockSpec((n_sel, LANES), lambda b, g, i: (0, 0)),
        ],
        out_specs=[
            pl.BlockSpec((sq, rep * dh, qb), lambda b, g, i: (b, g, i)),
            pl.BlockSpec((sq, sq, n_sel, qb), lambda b, g, i: (b, g, 0, i)),
            pl.BlockSpec((sq, sq, sq, 8, LANES), lambda b, g, i: (b, g, i, 0, 0)),
        ],
        out_shape=[
            jax.ShapeDtypeStruct((batch, heads * dh, seq), F32),
            jax.ShapeDtypeStruct((batch, groups, n_sel, seq), F32),
            jax.ShapeDtypeStruct((batch, groups, nq, 8, LANES), F32),
        ],
        scratch_shapes=[pltpu.VMEM((n_sel, qb), F32), pltpu.VMEM((n_sel, qb), F32),
                        pltpu.VMEM((n_sel, qb), jnp.int32)],
        compiler_params=_params("parallel", "parallel", "parallel"),
        name="nsa_select",
    )(slopes, att, k_cmp, v_cmp_t, _nsa_overlap_t(nc_pad, n_sel), pool)
    tile_flags = (tile_hits[:, :, :, 0, :n_tiles] > 0.5).astype(jnp.int32).reshape(batch * groups * nq, 1, n_tiles)

    s_hi = slopes.astype(BF16)
    s_mid = (slopes - s_hi.astype(F32)).astype(BF16)
    s_lo = (slopes - s_hi.astype(F32) - s_mid.astype(F32)).astype(BF16)
    q_aug = jnp.zeros((heads, dh), BF16).at[:, 0].set(s_hi).at[:, 1].set(s_mid).at[:, 2].set(s_lo)
    q_aug = q_aug.reshape(groups, rep, dh)

    o_t = pl.pallas_call(
        _nsa_attend_kernel, grid=(batch, groups, nq),
        in_specs=[
            pl.BlockSpec(memory_space=pltpu.SMEM),
            pl.BlockSpec((sq, 1, n_tiles), lambda b, g, i: ((b * groups + g) * nq + i, 0, 0),
                         memory_space=pltpu.SMEM),
            pl.BlockSpec((qb, rep * dh), lambda b, g, i: (b * nq + i, g)),
            pl.BlockSpec((sq, rep, dh), lambda b, g, i: (g, 0, 0)),
            pl.BlockSpec((sq, seq, 2 * dh), lambda b, g, i: (g, b, 0)),
            pl.BlockSpec((sq, sq, dh, seq), lambda b, g, i: (b, g, 0, 0)),
            pl.BlockSpec((sq, seq, 2 * dh), lambda b, g, i: (g, b, 0)),
            pl.BlockSpec((sq, sq, dh, seq), lambda b, g, i: (b, g, 0, 0)),
            pl.BlockSpec((sq, sq, n_sel, qb), lambda b, g, i: (b, g, 0, i)),
            pl.BlockSpec((sq, rep * dh, qb), lambda b, g, i: (b, g, i)),
            pl.BlockSpec((sq, sq, rep * 3, qb), lambda b, g, i: (b, g, 0, i)),
        ],
        out_specs=pl.BlockSpec((sq, rep * dh, qb), lambda b, g, i: (b, g, i)),
        out_shape=jax.ShapeDtypeStruct((batch, heads * dh, seq), BF16),
        scratch_shapes=[pltpu.VMEM((1, rep * qb), F32), pltpu.VMEM((1, rep * qb), F32),
                        pltpu.VMEM((dh, rep * qb), F32), pltpu.SMEM((n_tiles,), jnp.int32)],
        compiler_params=_params("parallel", "parallel", "arbitrary"),
        name="nsa_attend",
    )(slopes, tile_flags, att, q_aug, k_sel_aug, v_sel_t, k_win_aug, v_win_t, sel_bias, o_c, gates_t)
    if defer_out:
        return o_t, w_out
    return _linear_t(o_t, w_out, h)


def kernel(x, norm_mix, norm_ffn, norm_final, w_ffn_gate, w_ffn_up, w_ffn_down, a_w_in, a_w_gate2, a_b_gate, a_g_norm, a_w_out, b_w_in, b_b_in, b_ln_g, b_ln_b, b_w_s, b_b_s, b_w_out, c_w_pw1, c_b_pw1, c_w_dw, c_b_dw, c_ln_g, c_ln_b, c_w_pw2, c_b_pw2, d_w_in, d_pe_k, d_w_ck1, d_w_ck2, d_pe_v, d_w_cv1, d_w_cv2, d_w_out):
    batch, seq, d = x.shape
    depth = norm_mix.shape[0]
    h = x.reshape(batch * seq, d)
    for i in range(depth):
        m, j = i % 4, i // 4
        pending = None
        if m == 0:
            pending = _gla_mixer(h, norm_mix[i], a_w_in[j], a_w_gate2[j], a_b_gate[j], a_g_norm[j], a_w_out[j],
                                 batch, seq, defer_out=True)
        elif m == 1:
            h = _gmlp_mixer(h, norm_mix[i], b_w_in[j], b_b_in[j], b_ln_g[j], b_ln_b[j], b_w_s[j], b_b_s[j],
                            b_w_out[j])
        elif m == 2:
            h = _conv_mixer(h, norm_mix[i], c_w_pw1[j], c_b_pw1[j], c_w_dw[j], c_b_dw[j], c_ln_g[j],
                            c_ln_b[j], c_w_pw2[j], c_b_pw2[j], batch, seq)
        else:
            pending = _nsa_mixer(h, norm_mix[i], d_w_in[j], d_pe_k[j], d_w_ck1[j], d_w_ck2[j], d_pe_v[j],
                                 d_w_cv1[j], d_w_cv2[j], d_w_out[j], batch, seq, defer_out=True)
        final = norm_final if i == depth - 1 else None
        h = _ffn(h, norm_ffn[i], w_ffn_gate, w_ffn_up, w_ffn_down, final_gain=final, mixer=pending, layer=i)
    return h.reshape(batch, seq, d)
```

```python
import functools

import jax
import jax.numpy as jnp
from jax import lax
from jax.experimental import pallas as pl
from jax.experimental.pallas import tpu as pltpu

F32 = jnp.float32
BF16 = jnp.bfloat16

EPS = 1e-6
V7X_VMEM_BYTES = 64 * 1024 * 1024
VMEM_LIMIT_BYTES = V7X_VMEM_BYTES - 8 * 1024 * 1024
LANES = 128
SUBLANES = 8
GLA_HEADS = 4
GLA_TAU = 16.0
GLA_CHUNK = 64
GLA_SUB = 16
GMLP_GROUPS = 8
GMLP_CHUNK = 128
CONV_WIDTH = 31
CONV_HALO = 32
NSA_HEADS = 16
NSA_KV_HEADS = 4
NSA_HEAD_DIM = 64
NSA_CMP_BLOCK = 32
NSA_CMP_STRIDE = 16
NSA_SEL_BLOCK = 64
NSA_TOP_N = 16
NSA_WINDOW = 512
NSA_Q_BLOCK = 128
NSA_KEY_TILE = 256
NSA_CMP_CHUNK = 128
NSA_RANK_CHUNK = 32
NSA_RANK_ROWS = 64
NEG_BIG = -1e30

NT_DIMS = (((1,), (1,)), ((), ()))
TN_DIMS = (((0,), (0,)), ((), ()))


def _params(*sem):
    return pltpu.CompilerParams(dimension_semantics=sem, vmem_limit_bytes=VMEM_LIMIT_BYTES)


def _dot(a, b):
    return jnp.dot(a, b, preferred_element_type=F32)


def _dot_nt(a, b):
    return lax.dot_general(a, b, NT_DIMS, preferred_element_type=F32)


def _dot_tn(a, b):
    return lax.dot_general(a, b, TN_DIMS, preferred_element_type=F32)


def _sigmoid(x):
    return 1.0 / (1.0 + jnp.exp(-x))


def _silu(x):
    return x * _sigmoid(x)


def _gelu_tanh(x):
    a = -2.0 * 0.7978845608028654 * 1.4426950408889634
    b = a * 0.044715
    return x / (1.0 + jnp.exp2(x * (a + b * (x * x))))


def _log_sigmoid(x):
    return jnp.minimum(x, 0.0) - jnp.log(1.0 + jnp.exp(-jnp.abs(x)))


def _rms(x, gain):
    ms = jnp.mean(x * x, axis=-1, keepdims=True)
    return x * lax.rsqrt(ms + EPS) * gain


def _layer_norm(x, gain, bias):
    mu = jnp.mean(x, axis=-1, keepdims=True)
    xc = x - mu
    var = jnp.mean(xc * xc, axis=-1, keepdims=True)
    return xc * lax.rsqrt(var + EPS) * gain + bias


def _split_bf16(x):
    hi = x.astype(BF16)
    lo = (x - hi.astype(F32)).astype(BF16)
    return hi, lo


def _row(v):
    return v.reshape(1, -1)


def _linear_kernel(*refs, has_gain, has_bias, has_res, act):
    refs = list(refs)
    x_ref = refs.pop(0)
    g_ref = refs.pop(0) if has_gain else None
    w_ref = refs.pop(0)
    b_ref = refs.pop(0) if has_bias else None
    r_ref = refs.pop(0) if has_res else None
    o_ref = refs.pop(0)

    if has_gain:
        xn = _rms(x_ref[...].astype(F32), g_ref[...]).astype(BF16)
    else:
        xn = x_ref[...].astype(BF16)
    y = _dot(xn, w_ref[...])
    if has_bias:
        y = y + b_ref[...]
    if act == "gelu":
        y = _gelu_tanh(y)
    elif act == "sigmoid":
        y = _sigmoid(y)
    if has_res:
        y = y + r_ref[...]
    o_ref[...] = y.astype(o_ref.dtype)


def _resident(shape):
    return pl.BlockSpec(shape, lambda *_: (0,) * len(shape), pipeline_mode=pl.Buffered(1))


def _linear(x, w, *, gain=None, bias=None, residual=None, act=None, out_dtype=F32, tm=512):
    m, k = x.shape
    n = w.shape[1]
    tm = min(tm, m)
    in_specs = [pl.BlockSpec((tm, k), lambda i: (i, 0))]
    args = [x]
    if gain is not None:
        in_specs.append(_resident((1, k)))
        args.append(_row(gain))
    in_specs.append(_resident((k, n)))
    args.append(w.astype(BF16))
    if bias is not None:
        in_specs.append(_resident((1, n)))
        args.append(_row(bias))
    if residual is not None:
        in_specs.append(pl.BlockSpec((tm, n), lambda i: (i, 0)))
        args.append(residual)
    kern = functools.partial(_linear_kernel, has_gain=gain is not None, has_bias=bias is not None,
                             has_res=residual is not None, act=act)
    return pl.pallas_call(
        kern, grid=(m // tm,), in_specs=in_specs,
        out_specs=pl.BlockSpec((tm, n), lambda i: (i, 0)),
        out_shape=jax.ShapeDtypeStruct((m, n), out_dtype),
        compiler_params=_params("parallel"),
        name="linear",
    )(*args)


def _linear_t_kernel(xt_ref, w_ref, r_ref, o_ref):
    o_ref[...] = r_ref[...] + _dot_tn(xt_ref[...], w_ref[...])


def _linear_t(x_t, w, residual, tm=512):
    batch, k, seq = x_t.shape
    n = w.shape[1]
    tm = min(tm, seq)
    nt = seq // tm
    return pl.pallas_call(
        _linear_t_kernel, grid=(batch, nt),
        in_specs=[
            pl.BlockSpec((pl.Squeezed(), k, tm), lambda b, i: (b, 0, i)),
            _resident((k, n)),
            pl.BlockSpec((tm, n), lambda b, i: (b * nt + i, 0)),
        ],
        out_specs=pl.BlockSpec((tm, n), lambda b, i: (b * nt + i, 0)),
        out_shape=jax.ShapeDtypeStruct((batch * seq, n), F32),
        compiler_params=_params("parallel", "parallel"),
        name="linear_t",
    )(x_t, w.astype(BF16), residual)


def _ffn_kernel(*refs, has_final, mixer_out):
    refs = list(refs)
    h_ref, g_ref, wg_ref, wu_ref, wd_ref = refs[:5]
    refs = refs[5:]
    fg_ref = refs.pop(0) if has_final else None
    mo_ref, wo_ref = (refs.pop(0), refs.pop(0)) if mixer_out else (None, None)
    o_ref = refs[-1]
    x = h_ref[...]
    if mixer_out == "rows":
        x = x + _dot(mo_ref[...], wo_ref[...])
    elif mixer_out == "features":
        x = x + _dot_tn(mo_ref[...], wo_ref[...])
    xn = _rms(x, g_ref[...]).astype(BF16)
    a = (_silu(_dot(xn, wg_ref[...])) * _dot(xn, wu_ref[...])).astype(BF16)
    y = x + _dot(a, wd_ref[...])
    if has_final:
        y = _rms(y, fg_ref[...])
    o_ref[...] = y


def _ffn(h, gain, w_gate, w_up, w_down, final_gain=None, mixer=None, layer=None, tm=512):
    m, d = h.shape
    ff = w_gate.shape[-1]
    tm = min(tm, m)
    if mixer is not None and mixer[0].ndim == 3:
        tm = min(tm, mixer[0].shape[2])

    def weight(shape):
        if layer is None:
            return _resident(shape)
        return pl.BlockSpec((pl.Squeezed(),) + shape, lambda *_: (layer, 0, 0), pipeline_mode=pl.Buffered(1))

    in_specs = [
        pl.BlockSpec((tm, d), lambda i: (i, 0)),
        _resident((1, d)), weight((d, ff)), weight((d, ff)), weight((ff, d)),
    ]
    args = [h, _row(gain), w_gate.astype(BF16), w_up.astype(BF16), w_down.astype(BF16)]
    if final_gain is not None:
        in_specs.append(_resident((1, d)))
        args.append(_row(final_gain))
    mixer_out = None
    if mixer is not None:
        mo, w_out = mixer
        if mo.ndim == 2:
            mixer_out = "rows"
            in_specs.append(pl.BlockSpec((tm, mo.shape[1]), lambda i: (i, 0)))
        else:
            mixer_out = "features"
            nt = mo.shape[2] // tm
            in_specs.append(pl.BlockSpec((pl.Squeezed(), mo.shape[1], tm), lambda i: (i // nt, 0, i % nt)))
        in_specs.append(_resident(w_out.shape))
        args += [mo, w_out.astype(BF16)]
    return pl.pallas_call(
        functools.partial(_ffn_kernel, has_final=final_gain is not None, mixer_out=mixer_out),
        grid=(m // tm,), in_specs=in_specs,
        out_specs=pl.BlockSpec((tm, d), lambda i: (i, 0)),
        out_shape=jax.ShapeDtypeStruct((m, d), F32),
        compiler_params=_params("parallel"),
        name="ffn",
    )(*args)


def _gla_decay_kernel(h_ref, g_ref, w1_ref, w2_ref, b_ref, o_ref):
    xn = _rms(h_ref[...], g_ref[...]).astype(BF16)
    lr = _dot(xn, w1_ref[...]).astype(BF16)
    x = _dot(lr, w2_ref[...]) + b_ref[...]
    o_ref[...] = _log_sigmoid(x) * (1.0 / GLA_TAU)


def _gla_decay(h, gain, w_lr, w_gate2, b_gate, tm=512):
    m, d = h.shape
    rank = w_lr.shape[1]
    n = w_gate2.shape[1]
    tm = min(tm, m)
    w1 = jnp.zeros((d, LANES), BF16).at[:, :rank].set(w_lr.astype(BF16))
    w2 = jnp.zeros((LANES, n), BF16).at[:rank, :].set(w_gate2.astype(BF16))
    return pl.pallas_call(
        _gla_decay_kernel, grid=(m // tm,),
        in_specs=[
            pl.BlockSpec((tm, d), lambda i: (i, 0)),
            pl.BlockSpec((1, d), lambda i: (0, 0)),
            pl.BlockSpec((d, LANES), lambda i: (0, 0)),
            pl.BlockSpec((LANES, n), lambda i: (0, 0)),
            pl.BlockSpec((1, n), lambda i: (0, 0)),
        ],
        out_specs=pl.BlockSpec((tm, n), lambda i: (i, 0)),
        out_shape=jax.ShapeDtypeStruct((m, n), F32),
        compiler_params=_params("parallel"),
        name="gla_decay",
    )(h, _row(gain), w1, w2, _row(b_gate))


def _gla_core_kernel(q_ref, k_ref, v_ref, go_ref, la_ref, gn_ref, o_ref, st_ref, *, n_chunks, scale):
    c_len, sub = GLA_CHUNK, GLA_SUB
    n_sub = c_len // sub
    tr = n_chunks * c_len

    @pl.when(pl.program_id(2) == 0)
    def _():
        st_ref[...] = jnp.zeros_like(st_ref)

    r_io = lax.broadcasted_iota(jnp.int32, (tr, tr), 0)
    c_io = lax.broadcasted_iota(jnp.int32, (tr, tr), 1)
    tril = jnp.where((r_io >= c_io) & (r_io // c_len == c_io // c_len), 1.0, 0.0).astype(BF16)
    hi, lo = _split_bf16(la_ref[...])
    bcum = _dot(tril, hi) + _dot(tril, lo)
    q = q_ref[...] * scale
    k = k_ref[...]
    v = v_ref[...].astype(BF16)
    dk = q.shape[1]
    qe = (q * jnp.exp(bcum)).astype(BF16)

    causal = [lax.broadcasted_iota(jnp.int32, (sub, (i + 1) * sub), 1)
              <= lax.broadcasted_iota(jnp.int32, (sub, (i + 1) * sub), 0) + i * sub for i in range(n_sub)]
    qs, ks, kdecs, blasts = [], [], [], []
    for c in range(n_chunks):
        bc = bcum[c * c_len:(c + 1) * c_len]
        qc = q[c * c_len:(c + 1) * c_len]
        kc = k[c * c_len:(c + 1) * c_len]
        for i in range(n_sub):
            r0, r1 = i * sub, (i + 1) * sub
            base = bc[r0 - 1:r0, :] if i else jnp.zeros((1, dk), F32)
            qs.append((qc[r0:r1] * jnp.exp(bc[r0:r1] - base)).astype(BF16))
            ks.append((kc[:r1] * jnp.exp(base - bc[:r1])).astype(BF16))
        blast = bc[c_len - 1:c_len, :]
        blasts.append(blast)
        kdecs.append((kc * jnp.exp(blast - bc)).astype(BF16))
    scores = [_dot_nt(qi, ki) for qi, ki in zip(qs, ks)]
    scores = [jnp.where(causal[n % n_sub], a, 0.0).astype(BF16) for n, a in enumerate(scores)]
    intra = [_dot(a, v[(n // n_sub) * c_len:(n // n_sub) * c_len + (n % n_sub + 1) * sub])
             for n, a in enumerate(scores)]
    updates = [_dot_tn(v[c * c_len:(c + 1) * c_len], kdecs[c]) for c in range(n_chunks)]
    st = st_ref[...]
    inter = []
    for c in range(n_chunks):
        inter.append(_dot_nt(qe[c * c_len:(c + 1) * c_len], st.astype(BF16)))
        st = st * jnp.exp(blasts[c]) + updates[c]
    st_ref[...] = st
    o = jnp.concatenate(inter, axis=0) + jnp.concatenate(intra, axis=0)
    o_ref[...] = (_rms(o, gn_ref[...]) * _silu(go_ref[...])).astype(o_ref.dtype)


def _gla_core(proj, log_a, g_norm, batch, seq, tr=512):
    t = proj.shape[0]
    h_n = GLA_HEADS
    dk = log_a.shape[1] // h_n
    dv = (proj.shape[1] - 2 * h_n * dk) // (2 * h_n)
    tr = min(tr, seq)
    nt = seq // tr
    kb, vb, gb = h_n, (2 * h_n * dk) // dv, (2 * h_n * dk) // dv + h_n
    kern = functools.partial(_gla_core_kernel, n_chunks=tr // GLA_CHUNK, scale=dk ** -0.5)
    return pl.pallas_call(
        kern, grid=(batch, h_n, nt),
        in_specs=[
            pl.BlockSpec((tr, dk), lambda b, h, i: (b * nt + i, h)),
            pl.BlockSpec((tr, dk), lambda b, h, i: (b * nt + i, kb + h)),
            pl.BlockSpec((tr, dv), lambda b, h, i: (b * nt + i, vb + h)),
            pl.BlockSpec((tr, dv), lambda b, h, i: (b * nt + i, gb + h)),
            pl.BlockSpec((tr, dk), lambda b, h, i: (b * nt + i, h)),
            pl.BlockSpec((1, dv), lambda b, h, i: (0, 0)),
        ],
        out_specs=pl.BlockSpec((tr, dv), lambda b, h, i: (b * nt + i, h)),
        out_shape=jax.ShapeDtypeStruct((t, h_n * dv), BF16),
        scratch_shapes=[pltpu.VMEM((dv, dk), F32)],
        compiler_params=_params("parallel", "parallel", "arbitrary"),
        name="gla_core",
    )(proj, proj, proj, proj, log_a, _row(g_norm))


def _gla_mixer(h, gain, w_in, w_gate2, b_gate, g_norm, w_out, batch, seq, defer_out=False):
    n_main = w_in.shape[1] - w_gate2.shape[0]
    proj = _linear(h, w_in[:, :n_main], gain=gain)
    log_a = _gla_decay(h, gain, w_in[:, n_main:], w_gate2, b_gate)
    o = _gla_core(proj, log_a, g_norm, batch, seq)
    if defer_out:
        return o, w_out
    return _linear(o, w_out, residual=h)


def _gmlp_sgu_kernel(u_ref, v_ref, lg_ref, lb_ref, ws_ref, bs_ref, wo_ref, h_ref, o_ref, p_ref, *, n_chunks):
    c_len = GMLP_CHUNK
    groups = ws_ref.shape[0]
    e = v_ref.shape[1]
    ge = e // groups
    causal = (lax.broadcasted_iota(jnp.int32, (c_len, c_len), 0)
              >= lax.broadcasted_iota(jnp.int32, (c_len, c_len), 1))
    for c in range(n_chunks):
        rows = slice(c * c_len, (c + 1) * c_len)
        vn = _layer_norm(v_ref[rows, :].astype(F32), lg_ref[...], lb_ref[...]).astype(BF16)
        for g in range(groups):
            cols = slice(g * ge, (g + 1) * ge)
            ws = jnp.where(causal, ws_ref[g], 0.0).astype(BF16)
            sv = _dot(ws, vn[:, cols]) + bs_ref[:, cols]
            p_ref[rows, cols] = (u_ref[rows, cols].astype(F32) * sv).astype(BF16)
    o_ref[...] = h_ref[...] + _dot(p_ref[...], wo_ref[...])


def _gmlp_mixer(h, gain, w_in, b_in, ln_g, ln_b, w_s, b_s, w_out, tm=512):
    m, d = h.shape
    e = w_out.shape[0]
    groups, c_len = w_s.shape[0], w_s.shape[1]
    z = _linear(h, w_in, gain=gain, bias=b_in, act="gelu", out_dtype=BF16)
    tm = min(tm, m)
    bs_full = jnp.repeat(b_s.T, e // groups, axis=1)
    return pl.pallas_call(
        functools.partial(_gmlp_sgu_kernel, n_chunks=tm // c_len),
        grid=(m // tm,),
        in_specs=[
            pl.BlockSpec((tm, e), lambda i: (i, 0)),
            pl.BlockSpec((tm, e), lambda i: (i, 1)),
            pl.BlockSpec((1, e), lambda i: (0, 0)),
            pl.BlockSpec((1, e), lambda i: (0, 0)),
            pl.BlockSpec((groups, c_len, c_len), lambda i: (0, 0, 0)),
            pl.BlockSpec((c_len, e), lambda i: (0, 0)),
            pl.BlockSpec((e, d), lambda i: (0, 0)),
            pl.BlockSpec((tm, d), lambda i: (i, 0)),
        ],
        out_specs=pl.BlockSpec((tm, d), lambda i: (i, 0)),
        out_shape=jax.ShapeDtypeStruct((m, d), F32),
        scratch_shapes=[pltpu.VMEM((tm, e), BF16)],
        compiler_params=_params("parallel"),
        name="gmlp_sgu",
    )(z, z, _row(ln_g), _row(ln_b), w_s, bs_full, w_out.astype(BF16), h)


def _conv_kernel(h_ref, halo_ref, g_ref, w1_ref, b1_ref, wdw_ref, bdw_ref, lg_ref, lb_ref, w2_ref, b2_ref,
                 o_ref, y_ref, c_ref, *, tm, rc, lc):
    d = h_ref.shape[1]
    halo = CONV_HALO

    def glu(x):
        ag = _dot(_rms(x, g_ref[...]).astype(BF16), w1_ref[...]) + b1_ref[...]
        return ag[:, :d] * _sigmoid(ag[:, d:])

    first = pl.program_id(1) == 0
    y_ref[0:halo, :] = jnp.where(first, 0.0, glu(halo_ref[...]))
    y_ref[halo:, :] = glu(h_ref[...])
    off = halo - (CONV_WIDTH - 1)
    sl = SUBLANES
    for r in range(tm // rc):
        for lb in range(d // lc):
            lanes = slice(lb * lc, (lb + 1) * lc)
            out = None
            for phase in range(sl):
                rows = rc + (sl if phase else 0)
                part = None
                for j in range(CONV_WIDTH):
                    if (off + j) % sl != phase:
                        continue
                    a = r * rc + (off + j) // sl * sl
                    win = y_ref[a:a + rows, lanes].reshape(rows // sl, sl, lc)
                    term = win * wdw_ref[sl * j:sl * (j + 1), lanes][None]
                    part = term if part is None else part + term
                part = part.reshape(rows, lc)[phase:phase + rc]
                out = part if out is None else out + part
            c_ref[r * rc:(r + 1) * rc, lanes] = out
    yc = _layer_norm(c_ref[...] + bdw_ref[...], lg_ref[...], lb_ref[...])
    o_ref[...] = h_ref[...] + _dot(_silu(yc).astype(BF16), w2_ref[...]) + b2_ref[...]


def _conv_mixer(h, gain, w_pw1, b_pw1, w_dw, b_dw, ln_g, ln_b, w_pw2, b_pw2, batch, seq, tm=512, rc=32, lc=256):
    m, d = h.shape
    tm = min(tm, seq)
    nt = seq // tm
    hb = tm // CONV_HALO
    kern = functools.partial(_conv_kernel, tm=tm, rc=rc, lc=lc)
    return pl.pallas_call(
        kern, grid=(batch, nt),
        in_specs=[
            pl.BlockSpec((tm, d), lambda b, i: (b * nt + i, 0)),
            pl.BlockSpec((CONV_HALO, d), lambda b, i: (jnp.maximum((b * nt + i) * hb - 1, 0), 0)),
            pl.BlockSpec((1, d), lambda b, i: (0, 0)),
            pl.BlockSpec((d, 2 * d), lambda b, i: (0, 0)),
            pl.BlockSpec((1, 2 * d), lambda b, i: (0, 0)),
            pl.BlockSpec((CONV_WIDTH * SUBLANES, d), lambda b, i: (0, 0)),
            pl.BlockSpec((1, d), lambda b, i: (0, 0)),
            pl.BlockSpec((1, d), lambda b, i: (0, 0)),
            pl.BlockSpec((1, d), lambda b, i: (0, 0)),
            pl.BlockSpec((d, d), lambda b, i: (0, 0)),
            pl.BlockSpec((1, d), lambda b, i: (0, 0)),
        ],
        out_specs=pl.BlockSpec((tm, d), lambda b, i: (b * nt + i, 0)),
        out_shape=jax.ShapeDtypeStruct((m, d), F32),
        scratch_shapes=[pltpu.VMEM((tm + CONV_HALO, d), F32), pltpu.VMEM((tm, d), F32)],
        compiler_params=_params("parallel", "arbitrary"),
        name="conv_mixer",
    )(h, h, _row(gain), w_pw1.astype(BF16), _row(b_pw1), jnp.repeat(w_dw, SUBLANES, axis=0),
      _row(b_dw), _row(ln_g), _row(ln_b),
      w_pw2.astype(BF16), _row(b_pw2))


def _nsa_compress_kernel(x_ref, pe_ref, w1_ref, w2_ref, w2t_ref, k_ref, vt_ref):
    stride, dh = NSA_CMP_STRIDE, x_ref.shape[1]
    ns = k_ref.shape[0]
    a = b = None
    for p in range(stride):
        xs = x_ref[pl.ds(p, ns, stride=stride), :]
        ta = _dot((xs + pe_ref[p:p + 1, :]).astype(BF16), w1_ref[p * dh:(p + 1) * dh, :])
        tb = _dot((xs + pe_ref[stride + p:stride + p + 1, :]).astype(BF16),
                  w1_ref[(stride + p) * dh:(stride + p + 1) * dh, :])
        a = ta if a is None else a + ta
        b = tb if b is None else b + tb
    b_next = jnp.concatenate([b[1:], b[:1]], axis=0)
    hid = _gelu_tanh(a + b_next).astype(BF16)
    k_ref[...] = _dot(hid, w2_ref[...]).astype(k_ref.dtype)
    vt_ref[...] = _dot_nt(w2t_ref[...], hid).astype(vt_ref.dtype)


def _nsa_compress(x, pe, w1, w2, batch):
    two, groups, t, dh = x.shape
    seq = t // batch
    ns = seq // NSA_CMP_STRIDE
    hid = w1.shape[2]
    w2b = w2.astype(BF16)
    w2t = jnp.swapaxes(w2, 1, 2).astype(BF16)
    sq = pl.Squeezed()
    return pl.pallas_call(
        _nsa_compress_kernel, grid=(two, batch, groups),
        in_specs=[
            pl.BlockSpec((sq, sq, seq, dh), lambda c, b, g: (c, g, b, 0)),
            pl.BlockSpec((sq, NSA_CMP_BLOCK, dh), lambda c, b, g: (c, 0, 0)),
            pl.BlockSpec((sq, NSA_CMP_BLOCK * dh, hid), lambda c, b, g: (c, 0, 0)),
            pl.BlockSpec((sq, hid, dh), lambda c, b, g: (c, 0, 0)),
            pl.BlockSpec((sq, dh, hid), lambda c, b, g: (c, 0, 0)),
        ],
        out_specs=[
            pl.BlockSpec((sq, sq, sq, ns, dh), lambda c, b, g: (c, b, g, 0, 0)),
            pl.BlockSpec((sq, sq, sq, dh, ns), lambda c, b, g: (c, b, g, 0, 0)),
        ],
        out_shape=[
            jax.ShapeDtypeStruct((two, batch, groups, ns, dh), BF16),
            jax.ShapeDtypeStruct((two, batch, groups, dh, ns), BF16),
        ],
        compiler_params=_params("parallel", "parallel", "parallel"),
        name="nsa_compress",
    )(x, pe, w1.astype(BF16), w2b, w2t)


def _nsa_select_kernel(sl_ref, q_ref, kc_ref, vct_ref, ovt_ref, pool_ref, oc_ref, selb_ref, flag_ref,
                       imp_ref, x_ref, cnt_ref, *, n_sel, top_n):
    qb, dh, sel_blk = NSA_Q_BLOCK, NSA_HEAD_DIM, NSA_SEL_BLOCK
    n_b = q_ref.shape[0]
    rep = q_ref.shape[2] // dh
    g_idx = pl.program_id(0)
    i_blk = pl.program_id(1)
    q0 = i_blk * qb
    nc_pad = kc_ref.shape[1]
    ch = min(NSA_CMP_CHUNK, nc_pad)

    qs = [jnp.concatenate([q_ref[b][:, r * dh:(r + 1) * dh] for r in range(rep)], axis=0)
          for b in range(n_b)]
    slopes = [sl_ref[g_idx * rep + r] for r in range(rep)]
    t_row = q0 + lax.broadcasted_iota(jnp.int32, (1, qb), 1)
    n_last = (q0 + qb - NSA_CMP_BLOCK) // NSA_CMP_STRIDE

    def compressed_branch(n_rows):
        n_col = lax.broadcasted_iota(jnp.int32, (n_rows, 1), 0)
        c_start = n_col * NSA_CMP_STRIDE
        mask = (c_start + (NSA_CMP_BLOCK - 1) <= t_row) & (n_col < nc_pad - 1)
        dist = t_row.astype(F32) - (c_start.astype(F32) + (NSA_CMP_BLOCK - 1) / 2.0)
        s_alls = [_dot_nt(kc_ref[b, 0:n_rows, :], qs[b]) for b in range(n_b)]
        p_alls, p_sums = [], []
        for b in range(n_b):
            p_sum = jnp.zeros((n_rows, qb), F32)
            p_parts = []
            for r in range(rep):
                s = jnp.where(mask, s_alls[b][:, r * qb:(r + 1) * qb] - slopes[r] * dist, NEG_BIG)
                m = jnp.max(s, axis=0, keepdims=True)
                e = jnp.where(mask, jnp.exp(s - m), 0.0)
                l = jnp.sum(e, axis=0, keepdims=True)
                p = e * jnp.where(l > 0.0, 1.0 / l, 0.0)
                p_sum = p_sum + p
                p_parts.append(p.astype(BF16))
            p_alls.append(jnp.concatenate(p_parts, axis=1))
            p_sums.append(_split_bf16(p_sum))
        o_cs = [_dot(vct_ref[b, :, 0:n_rows], p_alls[b]) for b in range(n_b)]
        imps = [_dot(ovt_ref[:, 0:n_rows], hi) + _dot(ovt_ref[:, 0:n_rows], lo) for hi, lo in p_sums]
        for b in range(n_b):
            for r in range(rep):
                oc_ref[b, r * dh:(r + 1) * dh, :] = o_cs[b][:, r * qb:(r + 1) * qb]
            imp_ref[b] = imps[b]

    n_variants = nc_pad // ch
    need = jnp.minimum(n_last // ch, n_variants - 1)
    for v in range(n_variants):
        @pl.when(need == v)
        def _():
            compressed_branch((v + 1) * ch)

    j_col = lax.broadcasted_iota(jnp.int32, (n_sel, 1), 0)
    cur = t_row // sel_blk
    forced = (j_col == 0) | (j_col == cur) | (j_col == cur - 1)
    valid = j_col * sel_blk <= t_row
    for b in range(n_b):
        x_ref[b] = jnp.where(forced, jnp.inf, jnp.where(valid, imp_ref[b], -jnp.inf))
    cnt_ref[...] = jnp.zeros(cnt_ref.shape, jnp.int32)
    n_live = (q0 + qb - 1) // sel_blk + 1
    ic = min(NSA_RANK_CHUNK, n_sel)
    rq = min(NSA_RANK_ROWS, n_sel)
    jj = lax.broadcasted_iota(jnp.int32, (SUBLANES, 1), 0)
    for c0 in range(0, n_sel, ic):
        @pl.when(c0 < n_live)
        def _():
            for r0 in range(0, n_sel, rq):
                @pl.when(r0 < n_live)
                def _():
                    groups = range(r0 // SUBLANES, (r0 + rq) // SUBLANES)
                    for b in range(n_b):
                        xg = {a: x_ref[b, SUBLANES * a:SUBLANES * (a + 1), :] for a in groups}
                        cnt = {a: cnt_ref[b, SUBLANES * a:SUBLANES * (a + 1), :] for a in groups}
                        for i in range(c0, c0 + ic):
                            xi = x_ref[b, i:i + 1, :]
                            for a in groups:
                                if SUBLANES * a > i:
                                    cnt[a] = cnt[a] + jnp.where(xi >= xg[a], 1, 0)
                                elif SUBLANES * a + SUBLANES - 1 < i:
                                    cnt[a] = cnt[a] + jnp.where(xi > xg[a], 1, 0)
                                else:
                                    tie = jnp.where(jj + SUBLANES * a > i, 1, 0)
                                    cnt[a] = cnt[a] + jnp.where(xi > xg[a], 1, jnp.where(xi == xg[a], tie, 0))
                        for a in groups:
                            cnt_ref[b, SUBLANES * a:SUBLANES * (a + 1), :] = cnt[a]
    chosen = [valid & (cnt_ref[b] < top_n) for b in range(n_b)]
    per_block = [_dot_nt(jnp.ones((SUBLANES, qb), BF16), jnp.where(c, 1.0, 0.0).astype(BF16)) for c in chosen]
    for b in range(n_b):
        selb_ref[b] = jnp.where(chosen[b], 0.0, NEG_BIG)
        flag_ref[b] = _dot(per_block[b].astype(BF16), pool_ref[...])


def _nsa_attend_kernel(sl_ref, flag_ref, q_ref, qa_ref, ks_ref, vst_ref, kw_ref, vwt_ref, selb_ref, oc_ref,
                       gt_ref, o_ref, m_ref, l_ref, acc_ref, list_ref):
    qb, dh, kt, sub = NSA_Q_BLOCK, NSA_HEAD_DIM, NSA_KEY_TILE, NSA_SEL_BLOCK
    nb = kt // sub
    rep = q_ref.shape[1] // dh
    g_idx = pl.program_id(1)
    q0 = pl.program_id(2) * qb
    kd = q0 // kt

    q_tile = q_ref[...]
    qp = jnp.concatenate(
        [jnp.concatenate([q_tile[:, r * dh:(r + 1) * dh], jnp.broadcast_to(qa_ref[r:r + 1, :], (qb, dh))], axis=1)
         for r in range(rep)], axis=0)
    slopes = [sl_ref[g_idx * rep + r] for r in range(rep)]
    d_rel = lax.broadcasted_iota(jnp.int32, (kt, qb), 1) - lax.broadcasted_iota(jnp.int32, (kt, qb), 0)

    def update(k_ref, vt_ref, specs, state):
        return absorb(vt_ref, specs, scores(k_ref, specs), state)

    def scores(k_ref, specs):
        return [_dot_nt(k_ref[pl.ds(pl.multiple_of(kb * kt, kt), kt), :], qp) for kb, _ in specs]

    def absorb(vt_ref, specs, raws, state):
        k0s = [pl.multiple_of(kb * kt, kt) for kb, _ in specs]
        offs = [(q0 - k0).astype(F32) for k0 in k0s]
        m_out, l_out, acc_out = [], [], []
        for r in range(rep):
            cols = slice(r * qb, (r + 1) * qb)
            cs = [-slopes[r] * off for off in offs]
            ss = [jnp.concatenate([raw[b * sub:(b + 1) * sub, cols] + bias_fn(b) for b in range(nb)], axis=0)
                  for raw, (_, bias_fn) in zip(raws, specs)]
            m_new = None if state is None else state[0][:, cols]
            for s, c in zip(ss, cs):
                top = jnp.max(s, axis=0, keepdims=True) + c
                m_new = top if m_new is None else jnp.maximum(m_new, top)
            l_new = acc = None
            if state is not None:
                alpha = jnp.exp(state[0][:, cols] - m_new)
                l_new = alpha * state[1][:, cols]
                acc = alpha * state[2][:, cols]
            for s, c, k0 in zip(ss, cs, k0s):
                e = jnp.exp(s - (m_new - c))
                e_sum = jnp.sum(e, axis=0, keepdims=True)
                l_new = e_sum if l_new is None else l_new + e_sum
                pv = _dot(vt_ref[:, pl.ds(k0, kt)], e.astype(BF16))
                acc = pv if acc is None else acc + pv
            m_out.append(m_new)
            l_out.append(l_new)
            acc_out.append(acc)
        return jnp.concatenate(m_out, axis=1), jnp.concatenate(l_out, axis=1), jnp.concatenate(acc_out, axis=1)

    def causal_bias(kb):
        return jnp.where(d_rel + (q0 - kb * kt) >= 0, 0.0, NEG_BIG)

    def sel_spec(kb, causal):
        if causal is None:
            return kb, lambda b: selb_ref[pl.ds(kb * nb + b, 1), :]
        return kb, lambda b: selb_ref[pl.ds(kb * nb + b, 1), :] + causal[b * sub:(b + 1) * sub]

    def win_spec(kb):
        dist = d_rel + (q0 - kb * kt)
        wb = jnp.where((dist >= 0) & (dist < NSA_WINDOW), 0.0, NEG_BIG)
        return kb, lambda b: wb[b * sub:(b + 1) * sub]

    near = jnp.where(kd > 0, kd - 1, 1)
    n_win = -(-NSA_WINDOW // kt) + 1
    first = jnp.clip(kd - (n_win - 1), 0, ks_ref.shape[0] // kt - n_win)
    near_specs = [sel_spec(kd, causal_bias(kd)), sel_spec(near, causal_bias(near))]
    win_specs = [win_spec(first + j) for j in range(n_win)]
    near_raw = scores(ks_ref, near_specs)
    win_raw = scores(kw_ref, win_specs)
    sel_state = absorb(vst_ref, near_specs, near_raw, None)
    _, l_w, acc_w = absorb(vwt_ref, win_specs, win_raw, None)
    o_w = acc_w * (1.0 / l_w)
    m_ref[...], l_ref[...], acc_ref[...] = sel_state

    def compact(kb, n):
        list_ref[n] = kb
        return n + (flag_ref[0, kb] > 0).astype(jnp.int32)

    n_act = lax.fori_loop(0, kd - 1, compact, jnp.int32(0))

    def step(kbs):
        state = update(ks_ref, vst_ref, [sel_spec(kb, None) for kb in kbs],
                       (m_ref[...], l_ref[...], acc_ref[...]))
        m_ref[...], l_ref[...], acc_ref[...] = state

    def pair(i, carry):
        step([list_ref[2 * i], list_ref[2 * i + 1]])
        return carry

    lax.fori_loop(0, n_act // 2, pair, 0)

    @pl.when(n_act % 2 == 1)
    def _():
        step([list_ref[n_act - 1]])

    o_s = acc_ref[...] * (1.0 / l_ref[...])

    for r in range(rep):
        cols = slice(r * qb, (r + 1) * qb)
        rows = slice(r * dh, (r + 1) * dh)
        g_c = gt_ref[3 * r:3 * r + 1, :]
        g_s = gt_ref[3 * r + 1:3 * r + 2, :]
        g_w = gt_ref[3 * r + 2:3 * r + 3, :]
        o_ref[rows, :] = (g_c * oc_ref[rows, :] + g_s * o_s[:, cols] + g_w * o_w[:, cols]).astype(o_ref.dtype)


def _alibi_slopes(n):
    return 2.0 ** (-8.0 * jnp.arange(1, n + 1, dtype=F32) / n)


def _nsa_overlap_t(nc_pad, n_sel):
    c_start = NSA_CMP_STRIDE * jnp.arange(nc_pad, dtype=jnp.int32)
    s_start = NSA_SEL_BLOCK * jnp.arange(n_sel, dtype=jnp.int32)
    ov = jnp.clip(jnp.minimum(c_start[:, None] + NSA_CMP_BLOCK, s_start[None] + NSA_SEL_BLOCK)
                  - jnp.maximum(c_start[:, None], s_start[None]), 0, None).astype(F32) / NSA_CMP_BLOCK
    return ov.T.astype(BF16)


def _nsa_proj_kernel(h_ref, g_ref, wqk_ref, wvt_ref, wc_ref, wgt_ref,
                     q_ref, ks_ref, kw_ref, vst_ref, vwt_ref, c_ref, gt_ref):
    groups, tm, dh2 = ks_ref.shape
    dh = dh2 // 2
    n_q = q_ref.shape[1]
    n_kv = groups * dh
    xn = _rms(h_ref[...], g_ref[...]).astype(BF16)
    y = _dot(xn, wqk_ref[...])
    q_ref[...] = y[:, :n_q].astype(q_ref.dtype)
    row = pl.program_id(0) * tm + lax.broadcasted_iota(jnp.int32, (tm, dh), 0)
    lane = lax.broadcasted_iota(jnp.int32, (tm, dh), 1)
    feat = jnp.where(lane < 3, (row % NSA_KEY_TILE).astype(F32), 0.0).astype(ks_ref.dtype)
    for g in range(groups):
        ks_ref[g, :, 0:dh] = y[:, n_q + g * dh:n_q + (g + 1) * dh].astype(ks_ref.dtype)
        ks_ref[g, :, dh:dh2] = feat
        kw_ref[g, :, 0:dh] = y[:, n_q + n_kv + g * dh:n_q + n_kv + (g + 1) * dh].astype(kw_ref.dtype)
        kw_ref[g, :, dh:dh2] = feat
    vt = _dot_nt(wvt_ref[...], xn)
    vst_ref[...] = vt[:n_kv].astype(vst_ref.dtype)
    vwt_ref[...] = vt[n_kv:].astype(vwt_ref.dtype)
    c = _dot(xn, wc_ref[...])
    for cg in range(2 * groups):
        c_ref[cg] = c[:, cg * dh:(cg + 1) * dh]
    gt_ref[...] = _sigmoid(_dot_nt(wgt_ref[...], xn))


def _nsa_proj(h, gain, w_in, batch, seq, tm=512):
    m, d = h.shape
    heads, groups, dh = NSA_HEADS, NSA_KV_HEADS, NSA_HEAD_DIM
    n_q, n_kv = heads * dh, groups * dh
    n_gate = w_in.shape[1] - (n_q + 6 * n_kv)
    tm = min(tm, seq)
    nt = seq // tm
    kv0 = n_q
    w_qk = jnp.concatenate([w_in[:, :n_q] * dh ** -0.5, w_in[:, kv0 + 2 * n_kv:kv0 + 3 * n_kv],
                            w_in[:, kv0 + 4 * n_kv:kv0 + 5 * n_kv]], axis=1).astype(BF16)
    w_vt = jnp.concatenate([w_in[:, kv0 + 3 * n_kv:kv0 + 4 * n_kv],
                            w_in[:, kv0 + 5 * n_kv:kv0 + 6 * n_kv]], axis=1).T.astype(BF16)
    w_c = w_in[:, kv0:kv0 + 2 * n_kv].astype(BF16)
    w_gt = w_in[:, kv0 + 6 * n_kv:].T.astype(BF16)
    sq = pl.Squeezed()
    return pl.pallas_call(
        _nsa_proj_kernel, grid=(m // tm,),
        in_specs=[
            pl.BlockSpec((tm, d), lambda i: (i, 0)),
            _resident((1, d)), _resident(w_qk.shape), _resident(w_vt.shape), _resident(w_c.shape),
            _resident(w_gt.shape),
        ],
        out_specs=[
            pl.BlockSpec((tm, n_q), lambda i: (i, 0)),
            pl.BlockSpec((groups, tm, 2 * dh), lambda i: (0, i, 0)),
            pl.BlockSpec((groups, tm, 2 * dh), lambda i: (0, i, 0)),
            pl.BlockSpec((sq, n_kv, tm), lambda i: (i // nt, 0, i % nt)),
            pl.BlockSpec((sq, n_kv, tm), lambda i: (i // nt, 0, i % nt)),
            pl.BlockSpec((2 * groups, tm, dh), lambda i: (0, i, 0)),
            pl.BlockSpec((sq, n_gate, tm), lambda i: (i // nt, 0, i % nt)),
        ],
        out_shape=[
            jax.ShapeDtypeStruct((m, n_q), BF16),
            jax.ShapeDtypeStruct((groups, m, 2 * dh), BF16),
            jax.ShapeDtypeStruct((groups, m, 2 * dh), BF16),
            jax.ShapeDtypeStruct((batch, n_kv, seq), BF16),
            jax.ShapeDtypeStruct((batch, n_kv, seq), BF16),
            jax.ShapeDtypeStruct((2 * groups, m, dh), F32),
            jax.ShapeDtypeStruct((batch, n_gate, seq), F32),
        ],
        compiler_params=_params("parallel"),
        name="nsa_proj",
    )(h, _row(gain), w_qk, w_vt, w_c, w_gt)


def _nsa_mixer(h, gain, w_in, pe_k, w_ck1, w_ck2, pe_v, w_cv1, w_cv2, w_out, batch, seq, defer_out=False):
    m, d = h.shape
    heads, groups, dh = NSA_HEADS, NSA_KV_HEADS, NSA_HEAD_DIM
    rep = heads // groups
    qb = NSA_Q_BLOCK
    n_q = heads * dh
    n_kv = groups * dh
    n_sel = seq // NSA_SEL_BLOCK
    nc_pad = seq // NSA_CMP_STRIDE
    top_n = min(NSA_TOP_N, n_sel)

    att, k_sel_aug, k_win_aug, v_sel_t, v_win_t, cmp_src, gates_t = _nsa_proj(h, gain, w_in, batch, seq)
    v_sel_t = v_sel_t.reshape(batch, groups, dh, seq)
    v_win_t = v_win_t.reshape(batch, groups, dh, seq)
    gates_t = gates_t.reshape(batch, groups, rep * 3, seq)
    k_cmp_both, v_cmp_t_both = _nsa_compress(
        cmp_src.reshape(2, groups, m, dh), jnp.stack([pe_k, pe_v]), jnp.stack([w_ck1, w_cv1]),
        jnp.stack([w_ck2, w_cv2]), batch)
    k_cmp, v_cmp_t = k_cmp_both[0], v_cmp_t_both[1]

    sq = pl.Squeezed()
    nq = seq // qb
    kt = NSA_KEY_TILE
    n_tiles = seq // kt
    slopes = _alibi_slopes(heads)
    pool = (jnp.arange(n_sel)[:, None] // (kt // NSA_SEL_BLOCK) == jnp.arange(LANES)[None, :]).astype(BF16)
    o_c, sel_bias, tile_hits = pl.pallas_call(
        functools.partial(_nsa_select_kernel, n_sel=n_sel, top_n=top_n),
        grid=(groups, nq),
        in_specs=[
            pl.BlockSpec(memory_space=pltpu.SMEM),
            pl.BlockSpec((batch, qb, rep * dh), lambda g, i: (0, i, g)),
            pl.BlockSpec((batch, sq, nc_pad, dh), lambda g, i: (0, g, 0, 0)),
            pl.BlockSpec((batch, sq, dh, nc_pad), lambda g, i: (0, g, 0, 0)),
            pl.BlockSpec((n_sel, nc_pad), lambda g, i: (0, 0)),
            pl.BlockSpec((n_sel, LANES), lambda g, i: (0, 0)),
        ],
        out_specs=[
            pl.BlockSpec((batch, rep * dh, qb), lambda g, i: (0, g, i)),
            pl.BlockSpec((batch, sq, n_sel, qb), lambda g, i: (0, g, 0, i)),
            pl.BlockSpec((batch, sq, sq, 8, LANES), lambda g, i: (0, g, i, 0, 0)),
        ],
        out_shape=[
            jax.ShapeDtypeStruct((batch, heads * dh, seq), F32),
            jax.ShapeDtypeStruct((batch, groups, n_sel, seq), F32),
            jax.ShapeDtypeStruct((batch, groups, nq, 8, LANES), F32),
        ],
        scratch_shapes=[pltpu.VMEM((batch, n_sel, qb), F32), pltpu.VMEM((batch, n_sel, qb), F32),
                        pltpu.VMEM((batch, n_sel, qb), jnp.int32)],
        compiler_params=_params("parallel", "parallel"),
        name="nsa_select",
    )(slopes, att.reshape(batch, seq, heads * dh), k_cmp, v_cmp_t, _nsa_overlap_t(nc_pad, n_sel), pool)
    tile_flags = (tile_hits[:, :, :, 0, :n_tiles] > 0.5).astype(jnp.int32).reshape(batch * groups * nq, 1, n_tiles)

    s_hi = slopes.astype(BF16)
    s_mid = (slopes - s_hi.astype(F32)).astype(BF16)
    s_lo = (slopes - s_hi.astype(F32) - s_mid.astype(F32)).astype(BF16)
    q_aug = jnp.zeros((heads, dh), BF16).at[:, 0].set(s_hi).at[:, 1].set(s_mid).at[:, 2].set(s_lo)
    q_aug = q_aug.reshape(groups, rep, dh)

    o_t = pl.pallas_call(
        _nsa_attend_kernel, grid=(batch, groups, nq),
        in_specs=[
            pl.BlockSpec(memory_space=pltpu.SMEM),
            pl.BlockSpec((sq, 1, n_tiles), lambda b, g, i: ((b * groups + g) * nq + i, 0, 0),
                         memory_space=pltpu.SMEM),
            pl.BlockSpec((qb, rep * dh), lambda b, g, i: (b * nq + i, g)),
            pl.BlockSpec((sq, rep, dh), lambda b, g, i: (g, 0, 0)),
            pl.BlockSpec((sq, seq, 2 * dh), lambda b, g, i: (g, b, 0)),
            pl.BlockSpec((sq, sq, dh, seq), lambda b, g, i: (b, g, 0, 0)),
            pl.BlockSpec((sq, seq, 2 * dh), lambda b, g, i: (g, b, 0)),
            pl.BlockSpec((sq, sq, dh, seq), lambda b, g, i: (b, g, 0, 0)),
            pl.BlockSpec((sq, sq, n_sel, qb), lambda b, g, i: (b, g, 0, i)),
            pl.BlockSpec((sq, rep * dh, qb), lambda b, g, i: (b, g, i)),
            pl.BlockSpec((sq, sq, rep * 3, qb), lambda b, g, i: (b, g, 0, i)),
        ],
        out_specs=pl.BlockSpec((sq, rep * dh, qb), lambda b, g, i: (b, g, i)),
        out_shape=jax.ShapeDtypeStruct((batch, heads * dh, seq), BF16),
        scratch_shapes=[pltpu.VMEM((1, rep * qb), F32), pltpu.VMEM((1, rep * qb), F32),
                        pltpu.VMEM((dh, rep * qb), F32), pltpu.SMEM((n_tiles,), jnp.int32)],
        compiler_params=_params("parallel", "parallel", "arbitrary"),
        name="nsa_attend",
    )(slopes, tile_flags, att, q_aug, k_sel_aug, v_sel_t, k_win_aug, v_win_t, sel_bias, o_c, gates_t)
    if defer_out:
        return o_t, w_out
    return _linear_t(o_t, w_out, h)


def kernel(x, norm_mix, norm_ffn, norm_final, w_ffn_gate, w_ffn_up, w_ffn_down, a_w_in, a_w_gate2, a_b_gate, a_g_norm, a_w_out, b_w_in, b_b_in, b_ln_g, b_ln_b, b_w_s, b_b_s, b_w_out, c_w_pw1, c_b_pw1, c_w_dw, c_b_dw, c_ln_g, c_ln_b, c_w_pw2, c_b_pw2, d_w_in, d_pe_k, d_w_ck1, d_w_ck2, d_pe_v, d_w_cv1, d_w_cv2, d_w_out):
    batch, seq, d = x.shape
    depth = norm_mix.shape[0]
    h = x.reshape(batch * seq, d)
    for i in range(depth):
        m, j = i % 4, i // 4
        pending = None
        if m == 0:
            pending = _gla_mixer(h, norm_mix[i], a_w_in[j], a_w_gate2[j], a_b_gate[j], a_g_norm[j], a_w_out[j],
                                 batch, seq, defer_out=True)
        elif m == 1:
            h = _gmlp_mixer(h, norm_mix[i], b_w_in[j], b_b_in[j], b_ln_g[j], b_ln_b[j], b_w_s[j], b_b_s[j],
                            b_w_out[j])
        elif m == 2:
            h = _conv_mixer(h, norm_mix[i], c_w_pw1[j], c_b_pw1[j], c_w_dw[j], c_b_dw[j], c_ln_g[j],
                            c_ln_b[j], c_w_pw2[j], c_b_pw2[j], batch, seq)
        else:
            pending = _nsa_mixer(h, norm_mix[i], d_w_in[j], d_pe_k[j], d_w_ck1[j], d_w_ck2[j], d_pe_v[j],
                                 d_w_cv1[j], d_w_cv2[j], d_w_out[j], batch, seq, defer_out=True)
        final = norm_final if i == depth - 1 else None
        h = _ffn(h, norm_ffn[i], w_ffn_gate, w_ffn_up, w_ffn_down, final_gain=final, mixer=pending, layer=i)
    return h.reshape(batch, seq, d)
```

```python
import functools

import jax
import jax.numpy as jnp
from jax import lax
from jax.experimental import pallas as pl
from jax.experimental.pallas import tpu as pltpu

F32 = jnp.float32
BF16 = jnp.bfloat16

EPS = 1e-6
V7X_VMEM_BYTES = 64 * 1024 * 1024
VMEM_LIMIT_BYTES = V7X_VMEM_BYTES - 8 * 1024 * 1024
LANES = 128
SUBLANES = 8
GLA_HEADS = 4
GLA_TAU = 16.0
GLA_CHUNK = 64
GLA_SUB = 16
GMLP_GROUPS = 8
GMLP_CHUNK = 128
CONV_WIDTH = 31
CONV_HALO = 32
NSA_HEADS = 16
NSA_KV_HEADS = 4
NSA_HEAD_DIM = 64
NSA_CMP_BLOCK = 32
NSA_CMP_STRIDE = 16
NSA_SEL_BLOCK = 64
NSA_TOP_N = 16
NSA_WINDOW = 512
NSA_Q_BLOCK = 128
NSA_KEY_TILE = 256
NSA_CMP_CHUNK = 128
NSA_RANK_CHUNK = 32
NSA_RANK_ROWS = 64
NEG_BIG = -1e30

NT_DIMS = (((1,), (1,)), ((), ()))
TN_DIMS = (((0,), (0,)), ((), ()))


def _params(*sem):
    return pltpu.CompilerParams(dimension_semantics=sem, vmem_limit_bytes=VMEM_LIMIT_BYTES)


def _dot(a, b):
    return jnp.dot(a, b, preferred_element_type=F32)


def _dot_nt(a, b):
    return lax.dot_general(a, b, NT_DIMS, preferred_element_type=F32)


def _dot_tn(a, b):
    return lax.dot_general(a, b, TN_DIMS, preferred_element_type=F32)


def _sigmoid(x):
    return 1.0 / (1.0 + jnp.exp(-x))


def _silu(x):
    return x * _sigmoid(x)


def _gelu_tanh(x):
    a = -2.0 * 0.7978845608028654 * 1.4426950408889634
    b = a * 0.044715
    return x / (1.0 + jnp.exp2(x * (a + b * (x * x))))


def _log_sigmoid(x):
    return jnp.minimum(x, 0.0) - jnp.log(1.0 + jnp.exp(-jnp.abs(x)))


def _rms(x, gain):
    ms = jnp.mean(x * x, axis=-1, keepdims=True)
    return x * lax.rsqrt(ms + EPS) * gain


def _layer_norm(x, gain, bias):
    mu = jnp.mean(x, axis=-1, keepdims=True)
    xc = x - mu
    var = jnp.mean(xc * xc, axis=-1, keepdims=True)
    return xc * lax.rsqrt(var + EPS) * gain + bias


def _split_bf16(x):
    hi = x.astype(BF16)
    lo = (x - hi.astype(F32)).astype(BF16)
    return hi, lo


def _row(v):
    return v.reshape(1, -1)


def _linear_kernel(*refs, has_gain, has_bias, has_res, act):
    refs = list(refs)
    x_ref = refs.pop(0)
    g_ref = refs.pop(0) if has_gain else None
    w_ref = refs.pop(0)
    b_ref = refs.pop(0) if has_bias else None
    r_ref = refs.pop(0) if has_res else None
    o_ref = refs.pop(0)

    if has_gain:
        xn = _rms(x_ref[...].astype(F32), g_ref[...]).astype(BF16)
    else:
        xn = x_ref[...].astype(BF16)
    y = _dot(xn, w_ref[...])
    if has_bias:
        y = y + b_ref[...]
    if act == "gelu":
        y = _gelu_tanh(y)
    elif act == "sigmoid":
        y = _sigmoid(y)
    if has_res:
        y = y + r_ref[...]
    o_ref[...] = y.astype(o_ref.dtype)


def _resident(shape):
    return pl.BlockSpec(shape, lambda *_: (0,) * len(shape), pipeline_mode=pl.Buffered(1))


def _linear(x, w, *, gain=None, bias=None, residual=None, act=None, out_dtype=F32, tm=512):
    m, k = x.shape
    n = w.shape[1]
    tm = min(tm, m)
    in_specs = [pl.BlockSpec((tm, k), lambda i: (i, 0))]
    args = [x]
    if gain is not None:
        in_specs.append(_resident((1, k)))
        args.append(_row(gain))
    in_specs.append(_resident((k, n)))
    args.append(w.astype(BF16))
    if bias is not None:
        in_specs.append(_resident((1, n)))
        args.append(_row(bias))
    if residual is not None:
        in_specs.append(pl.BlockSpec((tm, n), lambda i: (i, 0)))
        args.append(residual)
    kern = functools.partial(_linear_kernel, has_gain=gain is not None, has_bias=bias is not None,
                             has_res=residual is not None, act=act)
    return pl.pallas_call(
        kern, grid=(m // tm,), in_specs=in_specs,
        out_specs=pl.BlockSpec((tm, n), lambda i: (i, 0)),
        out_shape=jax.ShapeDtypeStruct((m, n), out_dtype),
        compiler_params=_params("parallel"),
        name="linear",
    )(*args)


def _linear_t_kernel(xt_ref, w_ref, r_ref, o_ref):
    o_ref[...] = r_ref[...] + _dot_tn(xt_ref[...], w_ref[...])


def _linear_t(x_t, w, residual, tm=512):
    batch, k, seq = x_t.shape
    n = w.shape[1]
    tm = min(tm, seq)
    nt = seq // tm
    return pl.pallas_call(
        _linear_t_kernel, grid=(batch, nt),
        in_specs=[
            pl.BlockSpec((pl.Squeezed(), k, tm), lambda b, i: (b, 0, i)),
            _resident((k, n)),
            pl.BlockSpec((tm, n), lambda b, i: (b * nt + i, 0)),
        ],
        out_specs=pl.BlockSpec((tm, n), lambda b, i: (b * nt + i, 0)),
        out_shape=jax.ShapeDtypeStruct((batch * seq, n), F32),
        compiler_params=_params("parallel", "parallel"),
        name="linear_t",
    )(x_t, w.astype(BF16), residual)


def _ffn_kernel(*refs, has_final, mixer_out):
    refs = list(refs)
    h_ref, g_ref, wg_ref, wu_ref, wd_ref = refs[:5]
    refs = refs[5:]
    fg_ref = refs.pop(0) if has_final else None
    mo_ref, wo_ref = (refs.pop(0), refs.pop(0)) if mixer_out else (None, None)
    o_ref = refs[-1]
    x = h_ref[...]
    if mixer_out == "rows":
        x = x + _dot(mo_ref[...], wo_ref[...])
    elif mixer_out == "features":
        x = x + _dot_tn(mo_ref[...], wo_ref[...])
    xn = _rms(x, g_ref[...]).astype(BF16)
    a = (_silu(_dot(xn, wg_ref[...])) * _dot(xn, wu_ref[...])).astype(BF16)
    y = x + _dot(a, wd_ref[...])
    if has_final:
        y = _rms(y, fg_ref[...])
    o_ref[...] = y


def _ffn(h, gain, w_gate, w_up, w_down, final_gain=None, mixer=None, layer=None, tm=512):
    m, d = h.shape
    ff = w_gate.shape[-1]
    tm = min(tm, m)
    if mixer is not None and mixer[0].ndim == 3:
        tm = min(tm, mixer[0].shape[2])

    def weight(shape):
        if layer is None:
            return _resident(shape)
        return pl.BlockSpec((pl.Squeezed(),) + shape, lambda *_: (layer, 0, 0), pipeline_mode=pl.Buffered(1))

    in_specs = [
        pl.BlockSpec((tm, d), lambda i: (i, 0)),
        _resident((1, d)), weight((d, ff)), weight((d, ff)), weight((ff, d)),
    ]
    args = [h, _row(gain), w_gate.astype(BF16), w_up.astype(BF16), w_down.astype(BF16)]
    if final_gain is not None:
        in_specs.append(_resident((1, d)))
        args.append(_row(final_gain))
    mixer_out = None
    if mixer is not None:
        mo, w_out = mixer
        if mo.ndim == 2:
            mixer_out = "rows"
            in_specs.append(pl.BlockSpec((tm, mo.shape[1]), lambda i: (i, 0)))
        else:
            mixer_out = "features"
            nt = mo.shape[2] // tm
            in_specs.append(pl.BlockSpec((pl.Squeezed(), mo.shape[1], tm), lambda i: (i // nt, 0, i % nt)))
        in_specs.append(_resident(w_out.shape))
        args += [mo, w_out.astype(BF16)]
    return pl.pallas_call(
        functools.partial(_ffn_kernel, has_final=final_gain is not None, mixer_out=mixer_out),
        grid=(m // tm,), in_specs=in_specs,
        out_specs=pl.BlockSpec((tm, d), lambda i: (i, 0)),
        out_shape=jax.ShapeDtypeStruct((m, d), F32),
        compiler_params=_params("parallel"),
        name="ffn",
    )(*args)


def _gla_decay_kernel(h_ref, g_ref, w1_ref, w2_ref, b_ref, o_ref):
    xn = _rms(h_ref[...], g_ref[...]).astype(BF16)
    lr = _dot(xn, w1_ref[...]).astype(BF16)
    x = _dot(lr, w2_ref[...]) + b_ref[...]
    o_ref[...] = _log_sigmoid(x) * (1.0 / GLA_TAU)


def _gla_decay(h, gain, w_lr, w_gate2, b_gate, tm=512):
    m, d = h.shape
    rank = w_lr.shape[1]
    n = w_gate2.shape[1]
    tm = min(tm, m)
    w1 = jnp.zeros((d, LANES), BF16).at[:, :rank].set(w_lr.astype(BF16))
    w2 = jnp.zeros((LANES, n), BF16).at[:rank, :].set(w_gate2.astype(BF16))
    return pl.pallas_call(
        _gla_decay_kernel, grid=(m // tm,),
        in_specs=[
            pl.BlockSpec((tm, d), lambda i: (i, 0)),
            pl.BlockSpec((1, d), lambda i: (0, 0)),
            pl.BlockSpec((d, LANES), lambda i: (0, 0)),
            pl.BlockSpec((LANES, n), lambda i: (0, 0)),
            pl.BlockSpec((1, n), lambda i: (0, 0)),
        ],
        out_specs=pl.BlockSpec((tm, n), lambda i: (i, 0)),
        out_shape=jax.ShapeDtypeStruct((m, n), F32),
        compiler_params=_params("parallel"),
        name="gla_decay",
    )(h, _row(gain), w1, w2, _row(b_gate))


def _gla_core_kernel(q_ref, k_ref, v_ref, go_ref, la_ref, gn_ref, o_ref, st_ref, *, n_chunks, scale):
    c_len, sub = GLA_CHUNK, GLA_SUB
    n_sub = c_len // sub
    tr = n_chunks * c_len

    @pl.when(pl.program_id(2) == 0)
    def _():
        st_ref[...] = jnp.zeros_like(st_ref)

    r_io = lax.broadcasted_iota(jnp.int32, (tr, tr), 0)
    c_io = lax.broadcasted_iota(jnp.int32, (tr, tr), 1)
    tril = jnp.where((r_io >= c_io) & (r_io // c_len == c_io // c_len), 1.0, 0.0).astype(BF16)
    hi, lo = _split_bf16(la_ref[...])
    bcum = _dot(tril, hi) + _dot(tril, lo)
    q = q_ref[...] * scale
    k = k_ref[...]
    v = v_ref[...].astype(BF16)
    dk = q.shape[1]
    qe = (q * jnp.exp(bcum)).astype(BF16)

    causal = [lax.broadcasted_iota(jnp.int32, (sub, (i + 1) * sub), 1)
              <= lax.broadcasted_iota(jnp.int32, (sub, (i + 1) * sub), 0) + i * sub for i in range(n_sub)]
    qs, ks, kdecs, blasts = [], [], [], []
    for c in range(n_chunks):
        bc = bcum[c * c_len:(c + 1) * c_len]
        qc = q[c * c_len:(c + 1) * c_len]
        kc = k[c * c_len:(c + 1) * c_len]
        for i in range(n_sub):
            r0, r1 = i * sub, (i + 1) * sub
            base = bc[r0 - 1:r0, :] if i else jnp.zeros((1, dk), F32)
            qs.append((qc[r0:r1] * jnp.exp(bc[r0:r1] - base)).astype(BF16))
            ks.append((kc[:r1] * jnp.exp(base - bc[:r1])).astype(BF16))
        blast = bc[c_len - 1:c_len, :]
        blasts.append(blast)
        kdecs.append((kc * jnp.exp(blast - bc)).astype(BF16))
    scores = [_dot_nt(qi, ki) for qi, ki in zip(qs, ks)]
    scores = [jnp.where(causal[n % n_sub], a, 0.0).astype(BF16) for n, a in enumerate(scores)]
    intra = [_dot(a, v[(n // n_sub) * c_len:(n // n_sub) * c_len + (n % n_sub + 1) * sub])
             for n, a in enumerate(scores)]
    updates = [_dot_tn(v[c * c_len:(c + 1) * c_len], kdecs[c]) for c in range(n_chunks)]
    st = st_ref[...]
    inter = []
    for c in range(n_chunks):
        inter.append(_dot_nt(qe[c * c_len:(c + 1) * c_len], st.astype(BF16)))
        st = st * jnp.exp(blasts[c]) + updates[c]
    st_ref[...] = st
    o = jnp.concatenate(inter, axis=0) + jnp.concatenate(intra, axis=0)
    o_ref[...] = (_rms(o, gn_ref[...]) * _silu(go_ref[...])).astype(o_ref.dtype)


def _gla_core(proj, log_a, g_norm, batch, seq, tr=512):
    t = proj.shape[0]
    h_n = GLA_HEADS
    dk = log_a.shape[1] // h_n
    dv = (proj.shape[1] - 2 * h_n * dk) // (2 * h_n)
    tr = min(tr, seq)
    nt = seq // tr
    kb, vb, gb = h_n, (2 * h_n * dk) // dv, (2 * h_n * dk) // dv + h_n
    kern = functools.partial(_gla_core_kernel, n_chunks=tr // GLA_CHUNK, scale=dk ** -0.5)
    return pl.pallas_call(
        kern, grid=(batch, h_n, nt),
        in_specs=[
            pl.BlockSpec((tr, dk), lambda b, h, i: (b * nt + i, h)),
            pl.BlockSpec((tr, dk), lambda b, h, i: (b * nt + i, kb + h)),
            pl.BlockSpec((tr, dv), lambda b, h, i: (b * nt + i, vb + h)),
            pl.BlockSpec((tr, dv), lambda b, h, i: (b * nt + i, gb + h)),
            pl.BlockSpec((tr, dk), lambda b, h, i: (b * nt + i, h)),
            pl.BlockSpec((1, dv), lambda b, h, i: (0, 0)),
        ],
        out_specs=pl.BlockSpec((tr, dv), lambda b, h, i: (b * nt + i, h)),
        out_shape=jax.ShapeDtypeStruct((t, h_n * dv), BF16),
        scratch_shapes=[pltpu.VMEM((dv, dk), F32)],
        compiler_params=_params("parallel", "parallel", "arbitrary"),
        name="gla_core",
    )(proj, proj, proj, proj, log_a, _row(g_norm))


def _gla_mixer(h, gain, w_in, w_gate2, b_gate, g_norm, w_out, batch, seq, defer_out=False):
    n_main = w_in.shape[1] - w_gate2.shape[0]
    proj = _linear(h, w_in[:, :n_main], gain=gain)
    log_a = _gla_decay(h, gain, w_in[:, n_main:], w_gate2, b_gate)
    o = _gla_core(proj, log_a, g_norm, batch, seq)
    if defer_out:
        return o, w_out
    return _linear(o, w_out, residual=h)


def _gmlp_sgu_kernel(u_ref, v_ref, lg_ref, lb_ref, ws_ref, bs_ref, wo_ref, h_ref, o_ref, p_ref, *, n_chunks):
    c_len = GMLP_CHUNK
    groups = ws_ref.shape[0]
    e = v_ref.shape[1]
    ge = e // groups
    causal = (lax.broadcasted_iota(jnp.int32, (c_len, c_len), 0)
              >= lax.broadcasted_iota(jnp.int32, (c_len, c_len), 1))
    for c in range(n_chunks):
        rows = slice(c * c_len, (c + 1) * c_len)
        vn = _layer_norm(v_ref[rows, :].astype(F32), lg_ref[...], lb_ref[...]).astype(BF16)
        for g in range(groups):
            cols = slice(g * ge, (g + 1) * ge)
            ws = jnp.where(causal, ws_ref[g], 0.0).astype(BF16)
            sv = _dot(ws, vn[:, cols]) + bs_ref[:, cols]
            p_ref[rows, cols] = (u_ref[rows, cols].astype(F32) * sv).astype(BF16)
    o_ref[...] = h_ref[...] + _dot(p_ref[...], wo_ref[...])


def _gmlp_mixer(h, gain, w_in, b_in, ln_g, ln_b, w_s, b_s, w_out, tm=512):
    m, d = h.shape
    e = w_out.shape[0]
    groups, c_len = w_s.shape[0], w_s.shape[1]
    z = _linear(h, w_in, gain=gain, bias=b_in, act="gelu", out_dtype=BF16)
    tm = min(tm, m)
    bs_full = jnp.repeat(b_s.T, e // groups, axis=1)
    return pl.pallas_call(
        functools.partial(_gmlp_sgu_kernel, n_chunks=tm // c_len),
        grid=(m // tm,),
        in_specs=[
            pl.BlockSpec((tm, e), lambda i: (i, 0)),
            pl.BlockSpec((tm, e), lambda i: (i, 1)),
            pl.BlockSpec((1, e), lambda i: (0, 0)),
            pl.BlockSpec((1, e), lambda i: (0, 0)),
            pl.BlockSpec((groups, c_len, c_len), lambda i: (0, 0, 0)),
            pl.BlockSpec((c_len, e), lambda i: (0, 0)),
            pl.BlockSpec((e, d), lambda i: (0, 0)),
            pl.BlockSpec((tm, d), lambda i: (i, 0)),
        ],
        out_specs=pl.BlockSpec((tm, d), lambda i: (i, 0)),
        out_shape=jax.ShapeDtypeStruct((m, d), F32),
        scratch_shapes=[pltpu.VMEM((tm, e), BF16)],
        compiler_params=_params("parallel"),
        name="gmlp_sgu",
    )(z, z, _row(ln_g), _row(ln_b), w_s, bs_full, w_out.astype(BF16), h)


def _conv_kernel(h_ref, halo_ref, g_ref, w1_ref, b1_ref, wdw_ref, bdw_ref, lg_ref, lb_ref, w2_ref, b2_ref,
                 o_ref, y_ref, c_ref, *, tm, rc, lc):
    d = h_ref.shape[1]
    halo = CONV_HALO

    def glu(x):
        ag = _dot(_rms(x, g_ref[...]).astype(BF16), w1_ref[...]) + b1_ref[...]
        return ag[:, :d] * _sigmoid(ag[:, d:])

    first = pl.program_id(1) == 0
    y_ref[0:halo, :] = jnp.where(first, 0.0, glu(halo_ref[...]))
    y_ref[halo:, :] = glu(h_ref[...])
    off = halo - (CONV_WIDTH - 1)
    sl = SUBLANES
    for r in range(tm // rc):
        for lb in range(d // lc):
            lanes = slice(lb * lc, (lb + 1) * lc)
            out = None
            for phase in range(sl):
                rows = rc + (sl if phase else 0)
                part = None
                for j in range(CONV_WIDTH):
                    if (off + j) % sl != phase:
                        continue
                    a = r * rc + (off + j) // sl * sl
                    win = y_ref[a:a + rows, lanes].reshape(rows // sl, sl, lc)
                    term = win * wdw_ref[sl * j:sl * (j + 1), lanes][None]
                    part = term if part is None else part + term
                part = part.reshape(rows, lc)[phase:phase + rc]
                out = part if out is None else out + part
            c_ref[r * rc:(r + 1) * rc, lanes] = out
    yc = _layer_norm(c_ref[...] + bdw_ref[...], lg_ref[...], lb_ref[...])
    o_ref[...] = h_ref[...] + _dot(_silu(yc).astype(BF16), w2_ref[...]) + b2_ref[...]


def _conv_mixer(h, gain, w_pw1, b_pw1, w_dw, b_dw, ln_g, ln_b, w_pw2, b_pw2, batch, seq, tm=512, rc=32, lc=256):
    m, d = h.shape
    tm = min(tm, seq)
    nt = seq // tm
    hb = tm // CONV_HALO
    kern = functools.partial(_conv_kernel, tm=tm, rc=rc, lc=lc)
    return pl.pallas_call(
        kern, grid=(batch, nt),
        in_specs=[
            pl.BlockSpec((tm, d), lambda b, i: (b * nt + i, 0)),
            pl.BlockSpec((CONV_HALO, d), lambda b, i: (jnp.maximum((b * nt + i) * hb - 1, 0), 0)),
            pl.BlockSpec((1, d), lambda b, i: (0, 0)),
            pl.BlockSpec((d, 2 * d), lambda b, i: (0, 0)),
            pl.BlockSpec((1, 2 * d), lambda b, i: (0, 0)),
            pl.BlockSpec((CONV_WIDTH * SUBLANES, d), lambda b, i: (0, 0)),
            pl.BlockSpec((1, d), lambda b, i: (0, 0)),
            pl.BlockSpec((1, d), lambda b, i: (0, 0)),
            pl.BlockSpec((1, d), lambda b, i: (0, 0)),
            pl.BlockSpec((d, d), lambda b, i: (0, 0)),
            pl.BlockSpec((1, d), lambda b, i: (0, 0)),
        ],
        out_specs=pl.BlockSpec((tm, d), lambda b, i: (b * nt + i, 0)),
        out_shape=jax.ShapeDtypeStruct((m, d), F32),
        scratch_shapes=[pltpu.VMEM((tm + CONV_HALO, d), F32), pltpu.VMEM((tm, d), F32)],
        compiler_params=_params("parallel", "arbitrary"),
        name="conv_mixer",
    )(h, h, _row(gain), w_pw1.astype(BF16), _row(b_pw1), jnp.repeat(w_dw, SUBLANES, axis=0),
      _row(b_dw), _row(ln_g), _row(ln_b),
      w_pw2.astype(BF16), _row(b_pw2))


def _nsa_compress_kernel(x_ref, pe_ref, w1_ref, w2_ref, w2t_ref, k_ref, vt_ref):
    stride, dh = NSA_CMP_STRIDE, x_ref.shape[1]
    ns = k_ref.shape[0]
    a = b = None
    for p in range(stride):
        xs = x_ref[pl.ds(p, ns, stride=stride), :]
        ta = _dot((xs + pe_ref[p:p + 1, :]).astype(BF16), w1_ref[p * dh:(p + 1) * dh, :])
        tb = _dot((xs + pe_ref[stride + p:stride + p + 1, :]).astype(BF16),
                  w1_ref[(stride + p) * dh:(stride + p + 1) * dh, :])
        a = ta if a is None else a + ta
        b = tb if b is None else b + tb
    b_next = jnp.concatenate([b[1:], b[:1]], axis=0)
    hid = _gelu_tanh(a + b_next).astype(BF16)
    k_ref[...] = _dot(hid, w2_ref[...]).astype(k_ref.dtype)
    vt_ref[...] = _dot_nt(w2t_ref[...], hid).astype(vt_ref.dtype)


def _nsa_compress(x, pe, w1, w2, batch):
    two, groups, t, dh = x.shape
    seq = t // batch
    ns = seq // NSA_CMP_STRIDE
    hid = w1.shape[2]
    w2b = w2.astype(BF16)
    w2t = jnp.swapaxes(w2, 1, 2).astype(BF16)
    sq = pl.Squeezed()
    return pl.pallas_call(
        _nsa_compress_kernel, grid=(two, batch, groups),
        in_specs=[
            pl.BlockSpec((sq, sq, seq, dh), lambda c, b, g: (c, g, b, 0)),
            pl.BlockSpec((sq, NSA_CMP_BLOCK, dh), lambda c, b, g: (c, 0, 0)),
            pl.BlockSpec((sq, NSA_CMP_BLOCK * dh, hid), lambda c, b, g: (c, 0, 0)),
            pl.BlockSpec((sq, hid, dh), lambda c, b, g: (c, 0, 0)),
            pl.BlockSpec((sq, dh, hid), lambda c, b, g: (c, 0, 0)),
        ],
        out_specs=[
            pl.BlockSpec((sq, sq, sq, ns, dh), lambda c, b, g: (c, b, g, 0, 0)),
            pl.BlockSpec((sq, sq, sq, dh, ns), lambda c, b, g: (c, b, g, 0, 0)),
        ],
        out_shape=[
            jax.ShapeDtypeStruct((two, batch, groups, ns, dh), BF16),
            jax.ShapeDtypeStruct((two, batch, groups, dh, ns), BF16),
        ],
        compiler_params=_params("parallel", "parallel", "parallel"),
        name="nsa_compress",
    )(x, pe, w1.astype(BF16), w2b, w2t)


def _nsa_select_kernel(sl_ref, q_ref, kc_ref, vct_ref, ovt_ref, pool_ref, oc_ref, selb_ref, flag_ref,
                       imp_ref, x_ref, cnt_ref, *, n_sel, top_n):
    qb, dh, sel_blk = NSA_Q_BLOCK, NSA_HEAD_DIM, NSA_SEL_BLOCK
    n_b = q_ref.shape[0]
    rep = q_ref.shape[2] // dh
    g_idx = pl.program_id(0)
    i_blk = pl.program_id(1)
    q0 = i_blk * qb
    nc_pad = kc_ref.shape[1]
    ch = min(NSA_CMP_CHUNK, nc_pad)

    qs = [jnp.concatenate([q_ref[b][:, r * dh:(r + 1) * dh] for r in range(rep)], axis=0)
          for b in range(n_b)]
    slopes = [sl_ref[g_idx * rep + r] for r in range(rep)]
    t_row = q0 + lax.broadcasted_iota(jnp.int32, (1, qb), 1)
    n_last = (q0 + qb - NSA_CMP_BLOCK) // NSA_CMP_STRIDE

    def compressed_branch(n_rows):
        n_col = lax.broadcasted_iota(jnp.int32, (n_rows, 1), 0)
        c_start = n_col * NSA_CMP_STRIDE
        mask = (c_start + (NSA_CMP_BLOCK - 1) <= t_row) & (n_col < nc_pad - 1)
        dist = t_row.astype(F32) - (c_start.astype(F32) + (NSA_CMP_BLOCK - 1) / 2.0)
        s_alls = [_dot_nt(kc_ref[b, 0:n_rows, :], qs[b]) for b in range(n_b)]
        p_alls, p_sums = [], []
        for b in range(n_b):
            p_sum = jnp.zeros((n_rows, qb), F32)
            p_parts = []
            for r in range(rep):
                s = jnp.where(mask, s_alls[b][:, r * qb:(r + 1) * qb] - slopes[r] * dist, NEG_BIG)
                m = jnp.max(s, axis=0, keepdims=True)
                e = jnp.where(mask, jnp.exp(s - m), 0.0)
                l = jnp.sum(e, axis=0, keepdims=True)
                p = e * jnp.where(l > 0.0, 1.0 / l, 0.0)
                p_sum = p_sum + p
                p_parts.append(p.astype(BF16))
            p_alls.append(jnp.concatenate(p_parts, axis=1))
            p_sums.append(_split_bf16(p_sum))
        o_cs = [_dot(vct_ref[b, :, 0:n_rows], p_alls[b]) for b in range(n_b)]
        imps = [_dot(ovt_ref[:, 0:n_rows], hi) + _dot(ovt_ref[:, 0:n_rows], lo) for hi, lo in p_sums]
        for b in range(n_b):
            for r in range(rep):
                oc_ref[b, r * dh:(r + 1) * dh, :] = o_cs[b][:, r * qb:(r + 1) * qb]
            imp_ref[b] = imps[b]

    n_variants = nc_pad // ch
    need = jnp.minimum(n_last // ch, n_variants - 1)
    for v in range(n_variants):
        @pl.when(need == v)
        def _():
            compressed_branch((v + 1) * ch)

    j_col = lax.broadcasted_iota(jnp.int32, (n_sel, 1), 0)
    cur = t_row // sel_blk
    forced = (j_col == 0) | (j_col == cur) | (j_col == cur - 1)
    valid = j_col * sel_blk <= t_row
    for b in range(n_b):
        x_ref[b] = jnp.where(forced, jnp.inf, jnp.where(valid, imp_ref[b], -jnp.inf))
    cnt_ref[...] = jnp.zeros(cnt_ref.shape, jnp.int32)
    n_live = (q0 + qb - 1) // sel_blk + 1
    ic = min(NSA_RANK_CHUNK, n_sel)
    rq = min(NSA_RANK_ROWS, n_sel)
    jj = lax.broadcasted_iota(jnp.int32, (SUBLANES, 1), 0)
    for c0 in range(0, n_sel, ic):
        @pl.when(c0 < n_live)
        def _():
            for r0 in range(0, n_sel, rq):
                @pl.when(r0 < n_live)
                def _():
                    groups = range(r0 // SUBLANES, (r0 + rq) // SUBLANES)
                    for b in range(n_b):
                        xg = {a: x_ref[b, SUBLANES * a:SUBLANES * (a + 1), :] for a in groups}
                        cnt = {a: cnt_ref[b, SUBLANES * a:SUBLANES * (a + 1), :] for a in groups}
                        for i in range(c0, c0 + ic):
                            xi = x_ref[b, i:i + 1, :]
                            for a in groups:
                                if SUBLANES * a > i:
                                    cnt[a] = cnt[a] + jnp.where(xi >= xg[a], 1, 0)
                                elif SUBLANES * a + SUBLANES - 1 < i:
                                    cnt[a] = cnt[a] + jnp.where(xi > xg[a], 1, 0)
                                else:
                                    tie = jnp.where(jj + SUBLANES * a > i, 1, 0)
                                    cnt[a] = cnt[a] + jnp.where(xi > xg[a], 1, jnp.where(xi == xg[a], tie, 0))
                        for a in groups:
                            cnt_ref[b, SUBLANES * a:SUBLANES * (a + 1), :] = cnt[a]
    chosen = [valid & (cnt_ref[b] < top_n) for b in range(n_b)]
    per_block = [_dot_nt(jnp.ones((SUBLANES, qb), BF16), jnp.where(c, 1.0, 0.0).astype(BF16)) for c in chosen]
    for b in range(n_b):
        selb_ref[b] = jnp.where(chosen[b], 0.0, NEG_BIG)
        flag_ref[b] = _dot(per_block[b].astype(BF16), pool_ref[...])


def _nsa_attend_kernel(sl_ref, flag_ref, q_ref, qa_ref, ks_ref, vst_ref, kw_ref, vwt_ref, selb_ref, oc_ref,
                       gt_ref, o_ref, m_ref, l_ref, acc_ref, list_ref):
    n_b = q_ref.shape[0]
    gens = [_attend_instance(b, sl_ref, flag_ref, q_ref.at[b], qa_ref, ks_ref.at[b], vst_ref.at[b], kw_ref.at[b],
                             vwt_ref.at[b], selb_ref.at[b], oc_ref.at[b], gt_ref.at[b], o_ref.at[b],
                             m_ref.at[b], l_ref.at[b], acc_ref.at[b], list_ref) for b in range(n_b)]
    for _ in range(2):
        for g in gens:
            next(g)
    for g in gens:
        for _ in g:
            pass


def _attend_instance(b_idx, sl_ref, flag_ref, q_ref, qa_ref, ks_ref, vst_ref, kw_ref, vwt_ref, selb_ref, oc_ref,
                     gt_ref, o_ref, m_ref, l_ref, acc_ref, list_ref):
    qb, dh, kt, sub = NSA_Q_BLOCK, NSA_HEAD_DIM, NSA_KEY_TILE, NSA_SEL_BLOCK
    nb = kt // sub
    rep = q_ref.shape[1] // dh
    g_idx = pl.program_id(0)
    q0 = pl.program_id(1) * qb
    kd = q0 // kt

    q_tile = q_ref[...]
    qp = jnp.concatenate(
        [jnp.concatenate([q_tile[:, r * dh:(r + 1) * dh], jnp.broadcast_to(qa_ref[r:r + 1, :], (qb, dh))], axis=1)
         for r in range(rep)], axis=0)
    slopes = [sl_ref[g_idx * rep + r] for r in range(rep)]
    d_rel = lax.broadcasted_iota(jnp.int32, (kt, qb), 1) - lax.broadcasted_iota(jnp.int32, (kt, qb), 0)

    def update(k_ref, vt_ref, specs, state):
        return absorb(vt_ref, specs, scores(k_ref, specs), state)

    def scores(k_ref, specs):
        return [_dot_nt(k_ref[pl.ds(pl.multiple_of(kb * kt, kt), kt), :], qp) for kb, _ in specs]

    def absorb(vt_ref, specs, raws, state):
        k0s = [pl.multiple_of(kb * kt, kt) for kb, _ in specs]
        offs = [(q0 - k0).astype(F32) for k0 in k0s]
        m_out, l_out, acc_out = [], [], []
        for r in range(rep):
            cols = slice(r * qb, (r + 1) * qb)
            cs = [-slopes[r] * off for off in offs]
            ss = [jnp.concatenate([raw[b * sub:(b + 1) * sub, cols] + bias_fn(b) for b in range(nb)], axis=0)
                  for raw, (_, bias_fn) in zip(raws, specs)]
            m_new = None if state is None else state[0][:, cols]
            for s, c in zip(ss, cs):
                top = jnp.max(s, axis=0, keepdims=True) + c
                m_new = top if m_new is None else jnp.maximum(m_new, top)
            l_new = acc = None
            if state is not None:
                alpha = jnp.exp(state[0][:, cols] - m_new)
                l_new = alpha * state[1][:, cols]
                acc = alpha * state[2][:, cols]
            for s, c, k0 in zip(ss, cs, k0s):
                e = jnp.exp(s - (m_new - c))
                e_sum = jnp.sum(e, axis=0, keepdims=True)
                l_new = e_sum if l_new is None else l_new + e_sum
                pv = _dot(vt_ref[:, pl.ds(k0, kt)], e.astype(BF16))
                acc = pv if acc is None else acc + pv
            m_out.append(m_new)
            l_out.append(l_new)
            acc_out.append(acc)
        return jnp.concatenate(m_out, axis=1), jnp.concatenate(l_out, axis=1), jnp.concatenate(acc_out, axis=1)

    def causal_bias(kb):
        return jnp.where(d_rel + (q0 - kb * kt) >= 0, 0.0, NEG_BIG)

    def sel_spec(kb, causal):
        if causal is None:
            return kb, lambda b: selb_ref[pl.ds(kb * nb + b, 1), :]
        return kb, lambda b: selb_ref[pl.ds(kb * nb + b, 1), :] + causal[b * sub:(b + 1) * sub]

    def win_spec(kb):
        dist = d_rel + (q0 - kb * kt)
        wb = jnp.where((dist >= 0) & (dist < NSA_WINDOW), 0.0, NEG_BIG)
        return kb, lambda b: wb[b * sub:(b + 1) * sub]

    near = jnp.where(kd > 0, kd - 1, 1)
    n_win = -(-NSA_WINDOW // kt) + 1
    first = jnp.clip(kd - (n_win - 1), 0, ks_ref.shape[0] // kt - n_win)
    near_specs = [sel_spec(kd, causal_bias(kd)), sel_spec(near, causal_bias(near))]
    win_specs = [win_spec(first + j) for j in range(n_win)]
    near_raw = scores(ks_ref, near_specs)
    win_raw = scores(kw_ref, win_specs)
    yield
    sel_state = absorb(vst_ref, near_specs, near_raw, None)
    _, l_w, acc_w = absorb(vwt_ref, win_specs, win_raw, None)
    o_w = acc_w * (1.0 / l_w)
    m_ref[...], l_ref[...], acc_ref[...] = sel_state
    yield

    def compact(kb, n):
        list_ref[b_idx, n] = kb
        return n + (flag_ref[b_idx, 0, kb] > 0).astype(jnp.int32)

    n_act = lax.fori_loop(0, kd - 1, compact, jnp.int32(0))

    def step(kbs):
        state = update(ks_ref, vst_ref, [sel_spec(kb, None) for kb in kbs],
                       (m_ref[...], l_ref[...], acc_ref[...]))
        m_ref[...], l_ref[...], acc_ref[...] = state

    def pair(i, carry):
        step([list_ref[b_idx, 2 * i], list_ref[b_idx, 2 * i + 1]])
        return carry

    lax.fori_loop(0, n_act // 2, pair, 0)

    @pl.when(n_act % 2 == 1)
    def _():
        step([list_ref[b_idx, n_act - 1]])

    o_s = acc_ref[...] * (1.0 / l_ref[...])

    for r in range(rep):
        cols = slice(r * qb, (r + 1) * qb)
        rows = slice(r * dh, (r + 1) * dh)
        g_c = gt_ref[3 * r:3 * r + 1, :]
        g_s = gt_ref[3 * r + 1:3 * r + 2, :]
        g_w = gt_ref[3 * r + 2:3 * r + 3, :]
        o_ref[rows, :] = (g_c * oc_ref[rows, :] + g_s * o_s[:, cols] + g_w * o_w[:, cols]).astype(o_ref.dtype)


def _alibi_slopes(n):
    return 2.0 ** (-8.0 * jnp.arange(1, n + 1, dtype=F32) / n)


def _nsa_overlap_t(nc_pad, n_sel):
    c_start = NSA_CMP_STRIDE * jnp.arange(nc_pad, dtype=jnp.int32)
    s_start = NSA_SEL_BLOCK * jnp.arange(n_sel, dtype=jnp.int32)
    ov = jnp.clip(jnp.minimum(c_start[:, None] + NSA_CMP_BLOCK, s_start[None] + NSA_SEL_BLOCK)
                  - jnp.maximum(c_start[:, None], s_start[None]), 0, None).astype(F32) / NSA_CMP_BLOCK
    return ov.T.astype(BF16)


def _nsa_proj_kernel(h_ref, g_ref, wqk_ref, wvt_ref, wc_ref, wgt_ref,
                     q_ref, ks_ref, kw_ref, vst_ref, vwt_ref, c_ref, gt_ref):
    groups, tm, dh2 = ks_ref.shape
    dh = dh2 // 2
    n_q = q_ref.shape[1]
    n_kv = groups * dh
    xn = _rms(h_ref[...], g_ref[...]).astype(BF16)
    y = _dot(xn, wqk_ref[...])
    q_ref[...] = y[:, :n_q].astype(q_ref.dtype)
    row = pl.program_id(0) * tm + lax.broadcasted_iota(jnp.int32, (tm, dh), 0)
    lane = lax.broadcasted_iota(jnp.int32, (tm, dh), 1)
    feat = jnp.where(lane < 3, (row % NSA_KEY_TILE).astype(F32), 0.0).astype(ks_ref.dtype)
    for g in range(groups):
        ks_ref[g, :, 0:dh] = y[:, n_q + g * dh:n_q + (g + 1) * dh].astype(ks_ref.dtype)
        ks_ref[g, :, dh:dh2] = feat
        kw_ref[g, :, 0:dh] = y[:, n_q + n_kv + g * dh:n_q + n_kv + (g + 1) * dh].astype(kw_ref.dtype)
        kw_ref[g, :, dh:dh2] = feat
    vt = _dot_nt(wvt_ref[...], xn)
    vst_ref[...] = vt[:n_kv].astype(vst_ref.dtype)
    vwt_ref[...] = vt[n_kv:].astype(vwt_ref.dtype)
    c = _dot(xn, wc_ref[...])
    for cg in range(2 * groups):
        c_ref[cg] = c[:, cg * dh:(cg + 1) * dh]
    gt_ref[...] = _sigmoid(_dot_nt(wgt_ref[...], xn))


def _nsa_proj(h, gain, w_in, batch, seq, tm=512):
    m, d = h.shape
    heads, groups, dh = NSA_HEADS, NSA_KV_HEADS, NSA_HEAD_DIM
    n_q, n_kv = heads * dh, groups * dh
    n_gate = w_in.shape[1] - (n_q + 6 * n_kv)
    tm = min(tm, seq)
    nt = seq // tm
    kv0 = n_q
    w_qk = jnp.concatenate([w_in[:, :n_q] * dh ** -0.5, w_in[:, kv0 + 2 * n_kv:kv0 + 3 * n_kv],
                            w_in[:, kv0 + 4 * n_kv:kv0 + 5 * n_kv]], axis=1).astype(BF16)
    w_vt = jnp.concatenate([w_in[:, kv0 + 3 * n_kv:kv0 + 4 * n_kv],
                            w_in[:, kv0 + 5 * n_kv:kv0 + 6 * n_kv]], axis=1).T.astype(BF16)
    w_c = w_in[:, kv0:kv0 + 2 * n_kv].astype(BF16)
    w_gt = w_in[:, kv0 + 6 * n_kv:].T.astype(BF16)
    sq = pl.Squeezed()
    return pl.pallas_call(
        _nsa_proj_kernel, grid=(m // tm,),
        in_specs=[
            pl.BlockSpec((tm, d), lambda i: (i, 0)),
            _resident((1, d)), _resident(w_qk.shape), _resident(w_vt.shape), _resident(w_c.shape),
            _resident(w_gt.shape),
        ],
        out_specs=[
            pl.BlockSpec((tm, n_q), lambda i: (i, 0)),
            pl.BlockSpec((groups, tm, 2 * dh), lambda i: (0, i, 0)),
            pl.BlockSpec((groups, tm, 2 * dh), lambda i: (0, i, 0)),
            pl.BlockSpec((sq, n_kv, tm), lambda i: (i // nt, 0, i % nt)),
            pl.BlockSpec((sq, n_kv, tm), lambda i: (i // nt, 0, i % nt)),
            pl.BlockSpec((2 * groups, tm, dh), lambda i: (0, i, 0)),
            pl.BlockSpec((sq, n_gate, tm), lambda i: (i // nt, 0, i % nt)),
        ],
        out_shape=[
            jax.ShapeDtypeStruct((m, n_q), BF16),
            jax.ShapeDtypeStruct((groups, m, 2 * dh), BF16),
            jax.ShapeDtypeStruct((groups, m, 2 * dh), BF16),
            jax.ShapeDtypeStruct((batch, n_kv, seq), BF16),
            jax.ShapeDtypeStruct((batch, n_kv, seq), BF16),
            jax.ShapeDtypeStruct((2 * groups, m, dh), F32),
            jax.ShapeDtypeStruct((batch, n_gate, seq), F32),
        ],
        compiler_params=_params("parallel"),
        name="nsa_proj",
    )(h, _row(gain), w_qk, w_vt, w_c, w_gt)


def _nsa_mixer(h, gain, w_in, pe_k, w_ck1, w_ck2, pe_v, w_cv1, w_cv2, w_out, batch, seq, defer_out=False):
    m, d = h.shape
    heads, groups, dh = NSA_HEADS, NSA_KV_HEADS, NSA_HEAD_DIM
    rep = heads // groups
    qb = NSA_Q_BLOCK
    n_q = heads * dh
    n_kv = groups * dh
    n_sel = seq // NSA_SEL_BLOCK
    nc_pad = seq // NSA_CMP_STRIDE
    top_n = min(NSA_TOP_N, n_sel)

    att, k_sel_aug, k_win_aug, v_sel_t, v_win_t, cmp_src, gates_t = _nsa_proj(h, gain, w_in, batch, seq)
    v_sel_t = v_sel_t.reshape(batch, groups, dh, seq)
    v_win_t = v_win_t.reshape(batch, groups, dh, seq)
    gates_t = gates_t.reshape(batch, groups, rep * 3, seq)
    k_cmp_both, v_cmp_t_both = _nsa_compress(
        cmp_src.reshape(2, groups, m, dh), jnp.stack([pe_k, pe_v]), jnp.stack([w_ck1, w_cv1]),
        jnp.stack([w_ck2, w_cv2]), batch)
    k_cmp, v_cmp_t = k_cmp_both[0], v_cmp_t_both[1]

    sq = pl.Squeezed()
    nq = seq // qb
    kt = NSA_KEY_TILE
    n_tiles = seq // kt
    slopes = _alibi_slopes(heads)
    pool = (jnp.arange(n_sel)[:, None] // (kt // NSA_SEL_BLOCK) == jnp.arange(LANES)[None, :]).astype(BF16)
    o_c, sel_bias, tile_hits = pl.pallas_call(
        functools.partial(_nsa_select_kernel, n_sel=n_sel, top_n=top_n),
        grid=(groups, nq),
        in_specs=[
            pl.BlockSpec(memory_space=pltpu.SMEM),
            pl.BlockSpec((batch, qb, rep * dh), lambda g, i: (0, i, g)),
            pl.BlockSpec((batch, sq, nc_pad, dh), lambda g, i: (0, g, 0, 0)),
            pl.BlockSpec((batch, sq, dh, nc_pad), lambda g, i: (0, g, 0, 0)),
            pl.BlockSpec((n_sel, nc_pad), lambda g, i: (0, 0)),
            pl.BlockSpec((n_sel, LANES), lambda g, i: (0, 0)),
        ],
        out_specs=[
            pl.BlockSpec((batch, rep * dh, qb), lambda g, i: (0, g, i)),
            pl.BlockSpec((batch, sq, n_sel, qb), lambda g, i: (0, g, 0, i)),
            pl.BlockSpec((batch, sq, sq, 8, LANES), lambda g, i: (0, g, i, 0, 0)),
        ],
        out_shape=[
            jax.ShapeDtypeStruct((batch, heads * dh, seq), F32),
            jax.ShapeDtypeStruct((batch, groups, n_sel, seq), F32),
            jax.ShapeDtypeStruct((batch, groups, nq, 8, LANES), F32),
        ],
        scratch_shapes=[pltpu.VMEM((batch, n_sel, qb), F32), pltpu.VMEM((batch, n_sel, qb), F32),
                        pltpu.VMEM((batch, n_sel, qb), jnp.int32)],
        compiler_params=_params("parallel", "parallel"),
        name="nsa_select",
    )(slopes, att.reshape(batch, seq, heads * dh), k_cmp, v_cmp_t, _nsa_overlap_t(nc_pad, n_sel), pool)
    tile_flags = (tile_hits[:, :, :, 0, :n_tiles] > 0.5).astype(jnp.int32).reshape(batch * groups * nq, 1, n_tiles)

    s_hi = slopes.astype(BF16)
    s_mid = (slopes - s_hi.astype(F32)).astype(BF16)
    s_lo = (slopes - s_hi.astype(F32) - s_mid.astype(F32)).astype(BF16)
    q_aug = jnp.zeros((heads, dh), BF16).at[:, 0].set(s_hi).at[:, 1].set(s_mid).at[:, 2].set(s_lo)
    q_aug = q_aug.reshape(groups, rep, dh)

    o_t = pl.pallas_call(
        _nsa_attend_kernel, grid=(groups, nq),
        in_specs=[
            pl.BlockSpec(memory_space=pltpu.SMEM),
            pl.BlockSpec((batch, sq, 1, n_tiles), lambda g, i: (0, g * nq + i, 0, 0), memory_space=pltpu.SMEM),
            pl.BlockSpec((batch, qb, rep * dh), lambda g, i: (0, i, g)),
            pl.BlockSpec((sq, rep, dh), lambda g, i: (g, 0, 0)),
            pl.BlockSpec((sq, batch, seq, 2 * dh), lambda g, i: (g, 0, 0, 0)),
            pl.BlockSpec((batch, sq, dh, seq), lambda g, i: (0, g, 0, 0)),
            pl.BlockSpec((sq, batch, seq, 2 * dh), lambda g, i: (g, 0, 0, 0)),
            pl.BlockSpec((batch, sq, dh, seq), lambda g, i: (0, g, 0, 0)),
            pl.BlockSpec((batch, sq, n_sel, qb), lambda g, i: (0, g, 0, i)),
            pl.BlockSpec((batch, rep * dh, qb), lambda g, i: (0, g, i)),
            pl.BlockSpec((batch, sq, rep * 3, qb), lambda g, i: (0, g, 0, i)),
        ],
        out_specs=pl.BlockSpec((batch, rep * dh, qb), lambda g, i: (0, g, i)),
        out_shape=jax.ShapeDtypeStruct((batch, heads * dh, seq), BF16),
        scratch_shapes=[pltpu.VMEM((batch, 1, rep * qb), F32), pltpu.VMEM((batch, 1, rep * qb), F32),
                        pltpu.VMEM((batch, dh, rep * qb), F32), pltpu.SMEM((batch, n_tiles), jnp.int32)],
        compiler_params=_params("parallel", "arbitrary"),
        name="nsa_attend",
    )(slopes, tile_flags.reshape(batch, groups * nq, 1, n_tiles), att.reshape(batch, seq, heads * dh), q_aug,
      k_sel_aug.reshape(groups, batch, seq, 2 * dh), v_sel_t, k_win_aug.reshape(groups, batch, seq, 2 * dh),
      v_win_t, sel_bias, o_c, gates_t)
    if defer_out:
        return o_t, w_out
    return _linear_t(o_t, w_out, h)


def kernel(x, norm_mix, norm_ffn, norm_final, w_ffn_gate, w_ffn_up, w_ffn_down, a_w_in, a_w_gate2, a_b_gate, a_g_norm, a_w_out, b_w_in, b_b_in, b_ln_g, b_ln_b, b_w_s, b_b_s, b_w_out, c_w_pw1, c_b_pw1, c_w_dw, c_b_dw, c_ln_g, c_ln_b, c_w_pw2, c_b_pw2, d_w_in, d_pe_k, d_w_ck1, d_w_ck2, d_pe_v, d_w_cv1, d_w_cv2, d_w_out):
    batch, seq, d = x.shape
    depth = norm_mix.shape[0]
    h = x.reshape(batch * seq, d)
    for i in range(depth):
        m, j = i % 4, i // 4
        pending = None
        if m == 0:
            pending = _gla_mixer(h, norm_mix[i], a_w_in[j], a_w_gate2[j], a_b_gate[j], a_g_norm[j], a_w_out[j],
                                 batch, seq, defer_out=True)
        elif m == 1:
            h = _gmlp_mixer(h, norm_mix[i], b_w_in[j], b_b_in[j], b_ln_g[j], b_ln_b[j], b_w_s[j], b_b_s[j],
                            b_w_out[j])
        elif m == 2:
            h = _conv_mixer(h, norm_mix[i], c_w_pw1[j], c_b_pw1[j], c_w_dw[j], c_b_dw[j], c_ln_g[j],
                            c_ln_b[j], c_w_pw2[j], c_b_pw2[j], batch, seq)
        else:
            pending = _nsa_mixer(h, norm_mix[i], d_w_in[j], d_pe_k[j], d_w_ck1[j], d_w_ck2[j], d_pe_v[j],
                                 d_w_cv1[j], d_w_cv2[j], d_w_out[j], batch, seq, defer_out=True)
        final = norm_final if i == depth - 1 else None
        h = _ffn(h, norm_ffn[i], w_ffn_gate, w_ffn_up, w_ffn_down, final_gain=final, mixer=pending, layer=i)
    return h.reshape(batch, seq, d)
```
